```python
import jax, jax.numpy as jnp
from jax import lax
import numpy as np

D_MODEL = 1024
BATCH = 16
SEQ = 4096
DEPTH = 2

HEAD_DIM = 64
MOBA_HEADS = 4
MOBA_BLOCK = 256
MOBA_TOPK = 3
MOBA_Q_CHUNK = 32
FOX_HEADS = 4
ATTN_Q_BLOCK = 128
CONV_CH = 256
CONV_K = 31
MLA_HEADS = 4
MLA_Q_LORA = 256
MLA_KV_LORA = 128
MLA_NOPE = 64
MLA_ROPE = 32
MLA_V = 64
ROPE_THETA = 10000.0
BRANCH_W = 256
N_BRANCH = 4
N_GROUPS = 4
EXPERTS_PER_GROUP = 8
N_EXPERTS = N_GROUPS * EXPERTS_PER_GROUP
TOPK_IN_GROUP = 2
EXPERT_FF = 256
MOE_ROW_BLOCK = 128
NORM_EPS = 1e-6
NEG_INF = -1e30

MOBA_W = MOBA_HEADS * HEAD_DIM
FOX_W = FOX_HEADS * HEAD_DIM
IN_COLS = 3 * MOBA_W + 3 * FOX_W + FOX_HEADS + 2 * CONV_CH + MLA_Q_LORA + MLA_KV_LORA + MLA_ROPE

kernel_name = "hybrid_moba_fox_conv_mla_hmoe"


def rmsnorm(x, g):
    xf = x.astype(jnp.float32)
    y = xf * lax.rsqrt(jnp.mean(xf * xf, axis=-1, keepdims=True) + NORM_EPS)
    return (y * g.astype(jnp.float32)).astype(x.dtype)


def layernorm(x, g, b):
    xf = x.astype(jnp.float32)
    mu = jnp.mean(xf, axis=-1, keepdims=True)
    var = jnp.mean(jnp.square(xf - mu), axis=-1, keepdims=True)
    y = (xf - mu) * lax.rsqrt(var + NORM_EPS)
    return (y * g.astype(jnp.float32) + b.astype(jnp.float32)).astype(x.dtype)


def split_heads(t, n):
    b, s, _ = t.shape
    return t.reshape(b, s, n, -1).transpose(0, 2, 1, 3)


def merge_heads(t):
    b, h, s, d = t.shape
    return t.transpose(0, 2, 1, 3).reshape(b, s, h * d)


def rope(x, cos, sin):
    x1, x2 = jnp.split(x, 2, axis=-1)
    y = jnp.concatenate([x1 * cos - x2 * sin, x2 * cos + x1 * sin], axis=-1)
    return y.astype(x.dtype)


def moba_attention(q, k, v, slopes):
    b, h, s, dh = q.shape
    nb = -(-s // MOBA_BLOCK)
    sp = nb * MOBA_BLOCK
    n_sel = min(MOBA_TOPK, nb - 1)
    pad = ((0, 0), (0, 0), (0, sp - s), (0, 0))
    q, k, v = jnp.pad(q, pad), jnp.pad(k, pad), jnp.pad(v, pad)
    kb = k.reshape(b, h, nb, MOBA_BLOCK, dh)
    vb = v.reshape(b, h, nb, MOBA_BLOCK, dh)
    kmean = jnp.mean(kb.astype(jnp.float32), axis=3)
    scale = dh ** -0.5
    nc = sp // MOBA_Q_CHUNK
    qc = jnp.moveaxis(q.reshape(b, h, nc, MOBA_Q_CHUNK, dh), 2, 0)
    bi = jnp.arange(b)[:, None, None, None]
    hi = jnp.arange(h)[None, :, None, None]
    m = slopes[None, :, None, None]
    blk_ids = jnp.arange(nb)
    in_blk = jnp.arange(MOBA_BLOCK)

    def chunk(args):
        qi, ci = args
        t = ci * MOBA_Q_CHUNK + jnp.arange(MOBA_Q_CHUNK)
        cur = (ci * MOBA_Q_CHUNK) // MOBA_BLOCK
        k_own = lax.dynamic_slice_in_dim(k, cur * MOBA_BLOCK, MOBA_BLOCK, axis=2)
        v_own = lax.dynamic_slice_in_dim(v, cur * MOBA_BLOCK, MOBA_BLOCK, axis=2)
        dist_own = t[:, None] - (cur * MOBA_BLOCK + in_blk)[None, :]
        s_own = jnp.einsum("bhcd,bhjd->bhcj", qi, k_own).astype(jnp.float32) * scale
        s_own = jnp.where(dist_own >= 0, s_own - m * dist_own, NEG_INF)
        if n_sel == 0:
            p = jax.nn.softmax(s_own, axis=-1).astype(v.dtype)
            return jnp.einsum("bhcj,bhjd->bhcd", p, v_own)
        gate = jnp.einsum("bhcd,bhnd->bhcn", qi.astype(jnp.float32), kmean)
        gate = jnp.where(blk_ids < cur, gate, NEG_INF)
        _, sel = lax.top_k(gate, n_sel)
        valid = sel < cur
        kg = kb[bi, hi, sel]
        vg = vb[bi, hi, sel]
        dist_sel = t[:, None, None] - (sel[..., None] * MOBA_BLOCK + in_blk)
        s_sel = jnp.einsum("bhcd,bhckjd->bhckj", qi, kg).astype(jnp.float32) * scale - m[..., None] * dist_sel
        s_sel = jnp.where(valid[..., None], s_sel, NEG_INF)
        c = qi.shape[2]
        logits = jnp.concatenate([s_sel.reshape(b, h, c, n_sel * MOBA_BLOCK), s_own], axis=-1)
        p = jax.nn.softmax(logits, axis=-1).astype(v.dtype)
        p_sel = p[..., : n_sel * MOBA_BLOCK].reshape(b, h, c, n_sel, MOBA_BLOCK)
        p_own = p[..., n_sel * MOBA_BLOCK:]
        return jnp.einsum("bhckj,bhckjd->bhcd", p_sel, vg) + jnp.einsum("bhcj,bhjd->bhcd", p_own, v_own)

    out = lax.map(chunk, (qc, jnp.arange(nc)))
    out = jnp.moveaxis(out, 0, 2).reshape(b, h, sp, dh)
    return out[:, :, :s]


def causal_sweep(q, k, v, scale, cum_log_f=None):
    s = q.shape[2]
    outs = []
    for i in range(s // ATTN_Q_BLOCK):
        lo, hi = i * ATTN_Q_BLOCK, (i + 1) * ATTN_Q_BLOCK
        sc = jnp.einsum("bhqd,bhkd->bhqk", q[:, :, lo:hi], k[:, :, :hi]).astype(jnp.float32) * scale
        if cum_log_f is not None:
            sc = sc + cum_log_f[:, :, lo:hi, None] - cum_log_f[:, :, None, :hi]
        mask = jnp.arange(hi)[None, :] <= (lo + jnp.arange(ATTN_Q_BLOCK))[:, None]
        p = jax.nn.softmax(jnp.where(mask, sc, NEG_INF), axis=-1).astype(v.dtype)
        outs.append(jnp.einsum("bhqk,bhkd->bhqd", p, v[:, :, :hi]))
    return jnp.concatenate(outs, axis=2)


def token_mixer(xn, w_in, b_forget, conv_w, conv_b, conv_ln_g, conv_ln_b, mla_gq, mla_wuq, mla_gkv, mla_wukv,
                w_gate, b_gate, w_branch, w_out, slopes, cos, sin):
    b, s, _ = xn.shape
    splits = np.cumsum([3 * MOBA_W, 3 * FOX_W, FOX_HEADS, 2 * CONV_CH, MLA_Q_LORA, MLA_KV_LORA]).tolist()
    proj = xn @ w_in
    moba_qkv, fox_qkv, fox_f, conv_in, c_q, c_kv, k_rope = jnp.split(proj, splits, axis=-1)

    qa, ka, va = (split_heads(t, MOBA_HEADS) for t in jnp.split(moba_qkv, 3, axis=-1))
    y_a = merge_heads(moba_attention(qa, ka, va, slopes))

    qb, kb, vb = (split_heads(t, FOX_HEADS) for t in jnp.split(fox_qkv, 3, axis=-1))
    log_f = jax.nn.log_sigmoid((fox_f + b_forget).astype(jnp.float32))
    cum = jnp.cumsum(log_f, axis=1).transpose(0, 2, 1)
    y_b = merge_heads(causal_sweep(qb, kb, vb, HEAD_DIM ** -0.5, cum))

    u = conv_in[..., :CONV_CH] * jax.nn.sigmoid(conv_in[..., CONV_CH:])
    u = lax.conv_general_dilated(u, conv_w[:, None, :], window_strides=(1,), padding=[(CONV_K - 1, 0)],
                                 dimension_numbers=("NWC", "WIO", "NWC"), feature_group_count=CONV_CH) + conv_b
    y_c = jax.nn.silu(layernorm(u, conv_ln_g, conv_ln_b))

    cq = rmsnorm(c_q, mla_gq)
    qd = (cq @ mla_wuq).reshape(b, s, MLA_HEADS, MLA_NOPE + MLA_ROPE)
    q_nope, q_pe = qd[..., :MLA_NOPE], rope(qd[..., MLA_NOPE:], cos[:, None, :], sin[:, None, :])
    ckv = rmsnorm(c_kv, mla_gkv)
    kvd = (ckv @ mla_wukv).reshape(b, s, MLA_HEADS, MLA_NOPE + MLA_V)
    k_nope, vd = kvd[..., :MLA_NOPE], kvd[..., MLA_NOPE:]
    k_pe = jnp.broadcast_to(rope(k_rope, cos, sin)[:, :, None, :], (b, s, MLA_HEADS, MLA_ROPE))
    qh = jnp.concatenate([q_nope, q_pe], axis=-1).transpose(0, 2, 1, 3)
    kh = jnp.concatenate([k_nope, k_pe], axis=-1).transpose(0, 2, 1, 3)
    vh = vd.transpose(0, 2, 1, 3)
    y_d = merge_heads(causal_sweep(qh, kh, vh, (MLA_NOPE + MLA_ROPE) ** -0.5))

    merged = jnp.zeros_like(xn)
    for i, y in enumerate((y_a, y_b, y_c, y_d)):
        gate = jax.nn.sigmoid(xn @ w_gate[i] + b_gate[i])
        merged = merged + gate * (y @ w_branch[i])
    return merged @ w_out


def grouped_expert_ffn(xt, eid, wt, w1, w3, w2):
    t, d = xt.shape
    a = eid.shape[0]
    tok = jnp.arange(a) // TOPK_IN_GROUP
    order = jnp.argsort(eid)
    se, st, sw = eid[order], tok[order], wt[order]
    counts = jnp.bincount(eid, length=N_EXPERTS)
    starts = jnp.cumsum(counts) - counts
    pcounts = ((counts + MOE_ROW_BLOCK - 1) // MOE_ROW_BLOCK) * MOE_ROW_BLOCK
    pends = jnp.cumsum(pcounts)
    pstarts = pends - pcounts
    dest = pstarts[se] + jnp.arange(a) - starts[se]
    p_rows = a + N_EXPERTS * MOE_ROW_BLOCK
    n_blk = p_rows // MOE_ROW_BLOCK
    rows = jnp.zeros((p_rows, d), xt.dtype).at[dest].set(xt[st])
    blk_e = jnp.minimum(jnp.searchsorted(pends, jnp.arange(n_blk) * MOE_ROW_BLOCK, side="right"), N_EXPERTS - 1)

    def ffn(args):
        xb, e = args
        return (jax.nn.silu(xb @ w1[e]) * (xb @ w3[e])) @ w2[e]

    out = lax.map(ffn, (rows.reshape(n_blk, MOE_ROW_BLOCK, d), blk_e)).reshape(p_rows, d)
    y = out[dest] * sw[:, None].astype(out.dtype)
    return jax.ops.segment_sum(y, st, num_segments=t)


def hierarchical_moe(h, w_rg, b_rg, w_re, b_re, w1, w3, w2):
    b, s, d = h.shape
    xt = h.reshape(b * s, d)
    g_logits = (xt @ w_rg + b_rg).astype(jnp.float32)
    p_group = jax.nn.softmax(g_logits, axis=-1)
    _, g_sel = lax.top_k(g_logits, 1)
    e_logits = (xt @ w_re + b_re).astype(jnp.float32).reshape(-1, N_GROUPS, EXPERTS_PER_GROUP)
    e_in = jnp.take_along_axis(e_logits, g_sel[:, :, None], axis=1)[:, 0]
    top_l, top_i = lax.top_k(e_in, TOPK_IN_GROUP)
    wts = jax.nn.softmax(top_l, axis=-1) * jnp.take_along_axis(p_group, g_sel, axis=1)
    eid = g_sel * EXPERTS_PER_GROUP + top_i
    y = grouped_expert_ffn(xt, eid.reshape(-1), wts.reshape(-1), w1, w3, w2)
    return y.reshape(b, s, d)


def setup_inputs(seed: int = 0) -> dict:
    key = jax.random.key(seed)
    ks = jax.random.split(key, 26)
    f32 = jnp.float32
    nrm = lambda k, shape, sc: jax.random.normal(k, shape, f32) * sc
    L = DEPTH
    return {
        "x": nrm(ks[0], (BATCH, SEQ, D_MODEL), 1.0),
        "norm1_g": 1.0 + nrm(ks[1], (L, D_MODEL), 0.02),
        "w_in": nrm(ks[2], (L, D_MODEL, IN_COLS), D_MODEL ** -0.5),
        "b_forget": 4.0 + nrm(ks[3], (L, FOX_HEADS), 0.5),
        "conv_w": nrm(ks[4], (L, CONV_K, CONV_CH), CONV_K ** -0.5),
        "conv_b": nrm(ks[5], (L, CONV_CH), 0.02),
        "conv_ln_g": 1.0 + nrm(ks[6], (L, CONV_CH), 0.02),
        "conv_ln_b": nrm(ks[7], (L, CONV_CH), 0.02),
        "mla_gq": 1.0 + nrm(ks[8], (L, MLA_Q_LORA), 0.02),
        "mla_wuq": nrm(ks[9], (L, MLA_Q_LORA, MLA_HEADS * (MLA_NOPE + MLA_ROPE)), MLA_Q_LORA ** -0.5),
        "mla_gkv": 1.0 + nrm(ks[10], (L, MLA_KV_LORA), 0.02),
        "mla_wukv": nrm(ks[11], (L, MLA_KV_LORA, MLA_HEADS * (MLA_NOPE + MLA_V)), MLA_KV_LORA ** -0.5),
        "w_gate": nrm(ks[12], (L, N_BRANCH, D_MODEL, D_MODEL), D_MODEL ** -0.5),
        "b_gate": nrm(ks[13], (L, N_BRANCH, D_MODEL), 0.02),
        "w_branch": nrm(ks[14], (L, N_BRANCH, BRANCH_W, D_MODEL), BRANCH_W ** -0.5),
        "w_out": nrm(ks[15], (L, D_MODEL, D_MODEL), D_MODEL ** -0.5),
        "norm2_g": 1.0 + nrm(ks[16], (L, D_MODEL), 0.02),
        "w_router_group": nrm(ks[17], (L, D_MODEL, N_GROUPS), D_MODEL ** -0.5),
        "b_router_group": nrm(ks[18], (L, N_GROUPS), 0.01),
        "w_router_expert": nrm(ks[19], (L, D_MODEL, N_EXPERTS), D_MODEL ** -0.5),
        "b_router_expert": nrm(ks[20], (L, N_EXPERTS), 0.01),
        "w_exp_gate": nrm(ks[21], (L, N_EXPERTS, D_MODEL, EXPERT_FF), D_MODEL ** -0.5),
        "w_exp_up": nrm(ks[22], (L, N_EXPERTS, D_MODEL, EXPERT_FF), D_MODEL ** -0.5),
        "w_exp_down": nrm(ks[23], (L, N_EXPERTS, EXPERT_FF, D_MODEL), EXPERT_FF ** -0.5),
        "final_g": 1.0 + nrm(ks[24], (D_MODEL,), 0.02),
    }


def reference(x, norm1_g, w_in, b_forget, conv_w, conv_b, conv_ln_g, conv_ln_b, mla_gq, mla_wuq, mla_gkv,
              mla_wukv, w_gate, b_gate, w_branch, w_out, norm2_g, w_router_group, b_router_group,
              w_router_expert, b_router_expert, w_exp_gate, w_exp_up, w_exp_down, final_g):
    s = x.shape[1]
    slopes = jnp.exp2(-8.0 * (jnp.arange(MOBA_HEADS, dtype=jnp.float32) + 1.0) / MOBA_HEADS)
    inv_freq = ROPE_THETA ** (-jnp.arange(0, MLA_ROPE, 2, dtype=jnp.float32) / MLA_ROPE)
    ang = jnp.arange(s, dtype=jnp.float32)[:, None] * inv_freq[None, :]
    cos, sin = jnp.cos(ang), jnp.sin(ang)
    h = x
    for l in range(DEPTH):
        xn = rmsnorm(h, norm1_g[l])
        h = h + token_mixer(xn, w_in[l], b_forget[l], conv_w[l], conv_b[l], conv_ln_g[l], conv_ln_b[l],
                            mla_gq[l], mla_wuq[l], mla_gkv[l], mla_wukv[l], w_gate[l], b_gate[l],
                            w_branch[l], w_out[l], slopes, cos, sin)
        hn = rmsnorm(h, norm2_g[l])
        h = h + hierarchical_moe(hn, w_router_group[l], b_router_group[l], w_router_expert[l],
                                 b_router_expert[l], w_exp_gate[l], w_exp_up[l], w_exp_down[l])
    return rmsnorm(h, final_g)
```

```python
import functools

import numpy as np
import jax
import jax.numpy as jnp
from jax import lax
from jax.experimental import pallas as pl
from jax.experimental.pallas import tpu as pltpu

F32 = jnp.float32
BF16 = jnp.bfloat16
I32 = jnp.int32

HEAD_DIM = 64
N_HEADS = 4
ATT_W = N_HEADS * HEAD_DIM
MOBA_BLOCK = 256
MOBA_TOPK = 3
CONV_CH = 256
CONV_K = 31
MLA_Q_LORA = 256
MLA_KV_LORA = 128
MLA_NOPE = 64
MLA_ROPE = 32
MLA_V = 64
MLA_HEAD_PAD = 128
ROPE_THETA = 10000.0
BRANCH_W = 256
N_BRANCH = 4
N_GROUPS = 4
EXPERTS_PER_GROUP = 8
N_EXPERTS = N_GROUPS * EXPERTS_PER_GROUP
EXPERT_FF = 256
NORM_EPS = 1e-6
NEG_INF = -1e30

LANES = 128
SUBLANES = 8
VMEM_LIMIT = 56 * 1024 * 1024

TOKEN_TILE = 512
ATT_TQ = 128
ATT_TK = 256
CONV_CHUNK = 128
FFN_ROWS = 256

C_MOBA, C_FOX, C_CONV, C_CQ, C_CKV, C_SMALL, C_END = 0, 768, 1536, 2048, 2304, 2432, 2560
SM_ROPE = 0
SM_CUM = 32


def _cparams(sem):
    return pltpu.CompilerParams(dimension_semantics=sem, vmem_limit_bytes=VMEM_LIMIT)


def _rms(x, g):
    return x * lax.rsqrt(jnp.mean(x * x, axis=-1, keepdims=True) + NORM_EPS) * g


def _sigmoid(x):
    return 1.0 / (1.0 + jnp.exp(-x))


def _dot(a, b):
    return jnp.dot(a, b, preferred_element_type=F32)


def _dot_nt(a, b):
    return lax.dot_general(a, b, (((1,), (1,)), ((), ())), preferred_element_type=F32)


def _inproj_kernel(h_ref, g_ref, w_ref, bf_ref, moba_ref, fox_ref, conv_ref, cq_ref, ckv_ref,
                   small_ref, cumt_ref, carry_ref, *, tiles_per_seq):
    i = pl.program_id(0)
    tm = h_ref.shape[0]
    xn = _rms(h_ref[...], g_ref[...]).astype(BF16)

    def proj(a, b):
        return _dot(xn, w_ref[:, a:b])

    moba_ref[...] = proj(C_MOBA, C_FOX).astype(BF16)
    fox_ref[...] = proj(C_FOX, C_CONV).astype(BF16)
    conv_ref[...] = proj(C_CONV, C_CQ).astype(BF16)
    cq_ref[...] = proj(C_CQ, C_CKV).astype(BF16)
    ckv_ref[...] = proj(C_CKV, C_SMALL).astype(BF16)
    sm = proj(C_SMALL, C_END)

    f = sm + bf_ref[...]
    c = jnp.minimum(f, 0.0) - jnp.log(1.0 + jnp.exp(-jnp.abs(f)))
    row = lax.broadcasted_iota(I32, c.shape, 0)
    sh = 1
    while sh < tm:
        c = c + jnp.where(row >= sh, pltpu.roll(c, sh, 0), 0.0)
        sh *= 2

    @pl.when(i % tiles_per_seq == 0)
    def _():
        carry_ref[...] = jnp.zeros_like(carry_ref)

    c = c + carry_ref[...]
    carry_ref[...] = c[tm - 1:tm, :]
    lane = lax.broadcasted_iota(I32, c.shape, 1)
    small_ref[...] = jnp.where((lane >= SM_CUM) & (lane < SM_CUM + N_HEADS), c, sm)
    cumt_ref[0] = c.T[SM_CUM:SM_CUM + SUBLANES, :]


def _inproj(h, g, w, bf, *, seq):
    t, d = h.shape
    tm = min(TOKEN_TILE, seq)
    tiles_per_seq = seq // tm
    nb = t // seq
    row = lambda i: (i, 0)
    fixed = lambda i: (0, 0)
    outs = [(C_FOX - C_MOBA, BF16), (C_CONV - C_FOX, BF16), (C_CQ - C_CONV, BF16),
            (C_CKV - C_CQ, BF16), (C_SMALL - C_CKV, BF16), (C_END - C_SMALL, F32)]
    return pl.pallas_call(
        functools.partial(_inproj_kernel, tiles_per_seq=tiles_per_seq),
        grid=(t // tm,),
        in_specs=[pl.BlockSpec((tm, d), row), pl.BlockSpec((1, d), fixed),
                  pl.BlockSpec((d, C_END), fixed), pl.BlockSpec((1, LANES), fixed)],
        out_specs=[pl.BlockSpec((tm, wdt), row) for wdt, _ in outs]
        + [pl.BlockSpec((1, SUBLANES, tm), lambda i: (i // tiles_per_seq, 0, i % tiles_per_seq))],
        out_shape=[jax.ShapeDtypeStruct((t, wdt), dt) for wdt, dt in outs]
        + [jax.ShapeDtypeStruct((nb, SUBLANES, seq), F32)],
        scratch_shapes=[pltpu.VMEM((1, LANES), F32)],
        compiler_params=_cparams(("arbitrary",)),
        name="inproj",
    )(h, g, w, bf)


def _attn_kernel(*refs, mode, tq, tk, nblk, n_sel):
    if mode == "fox":
        q_ref, k_ref, v_ref, ccol_ref, crow_ref, o_ref = refs
    elif mode == "moba":
        q_ref, k_ref, v_ref, o_ref, kmean_ref = refs
    else:
        q_ref, k_ref, v_ref, o_ref = refs
    i = pl.program_id(1)
    nh = N_HEADS
    rows = nh * tq
    q = q_ref[0]
    lane_h = lax.shift_right_logical(lax.broadcasted_iota(I32, (tq, ATT_W), 1), 6)

    if mode != "mla":
        qs = jnp.concatenate([jnp.where(lane_h == h, q, jnp.zeros_like(q)) for h in range(nh)], axis=0)

    r_in = lax.broadcasted_iota(I32, (rows, tk), 0) & (tq - 1)
    c_in = lax.broadcasted_iota(I32, (rows, tk), 1)
    n_full = (i * tq) // tk

    if mode == "fox":
        cc = ccol_ref[0]
        c_col = jnp.concatenate([cc[:, SM_CUM + h:SM_CUM + h + 1] for h in range(nh)], axis=0)
    if mode == "moba":
        slopes = [2.0 ** (-8.0 * (h + 1) / nh) for h in range(nh)]
        slope_col = jnp.concatenate([jnp.full((tq, 1), s, F32) for s in slopes], axis=0)
        alibi_rel = slope_col * (r_in - c_in).astype(F32)
        nbp = kmean_ref.shape[0]

        @pl.when(i == 0)
        def _():
            kmean_ref[...] = jnp.zeros_like(kmean_ref)
            for n in range(nblk):
                kb = k_ref[0, n * MOBA_BLOCK:(n + 1) * MOBA_BLOCK, :].astype(F32)
                kmean_ref[n:n + 1, :] = jnp.mean(kb, axis=0, keepdims=True)

        km = kmean_ref[...]
        km_hi = km.astype(BF16)
        km_lo = (km - km_hi.astype(F32)).astype(BF16)
        gate = _dot_nt(km_hi, qs) + _dot_nt(km_lo, qs)
        n_iota = lax.broadcasted_iota(I32, gate.shape, 0)
        cur = (i * tq) // MOBA_BLOCK
        gate = jnp.where(n_iota < cur, gate, NEG_INF)
        rank = jnp.zeros(gate.shape, I32)
        for n2 in range(nblk):
            gb = gate[n2:n2 + 1, :]
            ahead = jnp.where(gb > gate, 1, jnp.where((gb == gate) & (n_iota > n2), 1, 0))
            rank = rank + ahead
        sel_bias = jnp.where((rank < n_sel) & (n_iota < cur), 0.0, NEG_INF)
        sel_bias = jnp.concatenate([sel_bias, jnp.zeros((LANES - nbp, rows), F32)], axis=0).T
        blk_lane = lax.broadcasted_iota(I32, (rows, LANES), 1)

    def step(j, carry, diagonal):
        m, l, acc = carry
        k0 = pl.multiple_of(j * tk, tk)
        kj = k_ref[0, pl.ds(k0, tk), :]
        vj = v_ref[0, pl.ds(k0, tk), :]
        if mode == "mla":
            s = jnp.concatenate(
                [_dot_nt(q[:, h * MLA_HEAD_PAD:(h + 1) * MLA_HEAD_PAD],
                         kj[:, h * MLA_HEAD_PAD:(h + 1) * MLA_HEAD_PAD]) for h in range(nh)], axis=0)
        else:
            s = _dot_nt(qs, kj)
        if mode == "fox":
            c_row = jnp.concatenate(
                [jnp.broadcast_to(crow_ref[0, h:h + 1, pl.ds(k0, tk)], (tq, tk)) for h in range(nh)], axis=0)
            s = s + c_col - c_row
        if mode == "moba":
            base = (i * tq - j * tk).astype(F32)
            col = -slope_col * base
            if not diagonal:
                col = col + jnp.sum(jnp.where(blk_lane == j, sel_bias, 0.0), axis=1, keepdims=True)
            s = s - alibi_rel + col
        if diagonal:
            s = jnp.where(c_in + j * tk <= r_in + i * tq, s, NEG_INF)
        m_new = jnp.maximum(m, jnp.max(s, axis=1, keepdims=True))
        alpha = jnp.exp(m - m_new)
        p = jnp.exp(s - m_new)
        l = alpha * l + jnp.sum(p, axis=1, keepdims=True)
        acc = alpha * acc + _dot(p.astype(BF16), vj)
        return m_new, l, acc

    init = (jnp.full((rows, 1), NEG_INF, F32), jnp.zeros((rows, 1), F32), jnp.zeros((rows, ATT_W), F32))
    carry = lax.fori_loop(0, n_full, lambda j, c: step(j, c, False), init)
    m, l, acc = step(n_full, carry, True)
    acc = acc * (1.0 / l)
    out = jnp.zeros((tq, ATT_W), F32)
    for h in range(nh):
        out = jnp.where(lane_h == h, acc[h * tq:(h + 1) * tq, :], out)
    o_ref[0] = out.astype(BF16)


def _attention(mode, q_arr, q_blk, k_arr, k_blk, v_arr, v_blk, *, batch, seq, extra=()):
    tq, tk = min(ATT_TQ, seq), min(ATT_TK, seq)
    dq = MLA_HEAD_PAD * N_HEADS if mode == "mla" else ATT_W
    nblk = seq // MOBA_BLOCK
    r3 = lambda a: a.reshape(batch, seq, a.shape[-1])
    in_specs = [pl.BlockSpec((1, tq, dq), lambda b, i: (b, i, q_blk)),
                pl.BlockSpec((1, seq, dq), lambda b, i: (b, 0, k_blk)),
                pl.BlockSpec((1, seq, ATT_W), lambda b, i: (b, 0, v_blk))]
    args = [r3(q_arr), r3(k_arr), r3(v_arr)]
    scratch = []
    if mode == "fox":
        small, cumt = extra
        in_specs += [pl.BlockSpec((1, tq, LANES), lambda b, i: (b, i, 0)),
                     pl.BlockSpec((1, SUBLANES, seq), lambda b, i: (b, 0, 0))]
        args += [r3(small), cumt]
    if mode == "moba":
        scratch = [pltpu.VMEM((-(-nblk // SUBLANES) * SUBLANES, ATT_W), F32)]
    out = pl.pallas_call(
        functools.partial(_attn_kernel, mode=mode, tq=tq, tk=tk, nblk=nblk, n_sel=min(MOBA_TOPK, nblk - 1)),
        grid=(batch, seq // tq),
        in_specs=in_specs,
        out_specs=pl.BlockSpec((1, tq, ATT_W), lambda b, i: (b, i, 0)),
        out_shape=jax.ShapeDtypeStruct((batch, seq, ATT_W), BF16),
        scratch_shapes=scratch,
        compiler_params=_cparams(("parallel", "arbitrary")),
        name="attn_" + mode,
    )(*args)
    return out.reshape(batch * seq, ATT_W)


def _conv_kernel(x_ref, w_ref, p_ref, o_ref, u_ref, *, chunk):
    seq = x_ref.shape[1]
    halo = 32
    u_ref[0:halo, :] = jnp.zeros((halo, CONV_CH), F32)

    def glu(c, _):
        r0 = pl.multiple_of(c * chunk, chunk)
        x = x_ref[0, pl.ds(r0, chunk), :].astype(F32)
        u_ref[pl.ds(halo + r0, chunk), :] = x[:, :CONV_CH] * _sigmoid(x[:, CONV_CH:])
        return 0

    lax.fori_loop(0, seq // chunk, glu, 0)
    bias, ln_g, ln_b = p_ref[0:1, :], p_ref[1:2, :], p_ref[2:3, :]

    def conv(c, _):
        r0 = pl.multiple_of(c * chunk, chunk)
        n = chunk + halo
        win = u_ref[pl.ds(r0, n), :]
        acc = jnp.zeros((chunk, CONV_CH), F32)
        for b in range(SUBLANES):
            shifted = win if b == 0 else pltpu.roll(win, n - b, 0)
            for a in range(halo // SUBLANES + 1):
                k = SUBLANES * a + b - (halo - (CONV_K - 1))
                if 0 <= k < CONV_K:
                    acc = acc + w_ref[k:k + 1, :] * shifted[SUBLANES * a:SUBLANES * a + chunk, :]
        acc = acc + bias
        mu = jnp.mean(acc, axis=-1, keepdims=True)
        xc = acc - mu
        var = jnp.mean(xc * xc, axis=-1, keepdims=True)
        y = xc * lax.rsqrt(var + NORM_EPS) * ln_g + ln_b
        o_ref[0, pl.ds(r0, chunk), :] = (y * _sigmoid(y)).astype(BF16)
        return 0

    lax.fori_loop(0, seq // chunk, conv, 0)


def _conv_module(conv_in, w, p, *, batch, seq):
    chunk = min(CONV_CHUNK, seq)
    out = pl.pallas_call(
        functools.partial(_conv_kernel, chunk=chunk),
        grid=(batch,),
        in_specs=[pl.BlockSpec((1, seq, 2 * CONV_CH), lambda b: (b, 0, 0)),
                  pl.BlockSpec((32, CONV_CH), lambda b: (0, 0)),
                  pl.BlockSpec((SUBLANES, CONV_CH), lambda b: (0, 0))],
        out_specs=pl.BlockSpec((1, seq, CONV_CH), lambda b: (b, 0, 0)),
        out_shape=jax.ShapeDtypeStruct((batch, seq, CONV_CH), BF16),
        scratch_shapes=[pltpu.VMEM((seq + 32, CONV_CH), F32)],
        compiler_params=_cparams(("parallel",)),
        name="conv",
    )(conv_in.reshape(batch, seq, 2 * CONV_CH), w, p)
    return out.reshape(batch * seq, CONV_CH)


def _mla_prep_kernel(cq_ref, ckv_ref, small_ref, cos_ref, sin_ref, gq_ref, gkv_ref, wq_ref, wqr_ref,
                     wk_ref, wv_ref, pk_ref, pkr_ref, q_ref, k_ref, v_ref, *, scale):
    cos, sin = cos_ref[...], sin_ref[...]
    cq = _rms(cq_ref[...].astype(F32), gq_ref[...]).astype(BF16)
    q = _dot(cq, wq_ref[...]) * cos + _dot(cq, wqr_ref[...]) * sin
    q_ref[...] = (q * scale).astype(BF16)
    ckv = _rms(ckv_ref[...].astype(F32), gkv_ref[...]).astype(BF16)
    kr = small_ref[...].astype(BF16)
    k = _dot(ckv, wk_ref[...]) + _dot(kr, pk_ref[...]) * cos + _dot(kr, pkr_ref[...]) * sin
    k_ref[...] = k.astype(BF16)
    v_ref[...] = _dot(ckv, wv_ref[...]).astype(BF16)


def _mla_prep(cq, ckv, small, cos_t, sin_t, gq, gkv, wq, wqr, wk, wv, pk, pkr, *, seq):
    t = cq.shape[0]
    tm = min(TOKEN_TILE, seq)
    tiles_per_seq = seq // tm
    wfull = MLA_HEAD_PAD * N_HEADS
    row = lambda i: (i, 0)
    pos = lambda i: (i % tiles_per_seq, 0)
    fixed = lambda i: (0, 0)
    return pl.pallas_call(
        functools.partial(_mla_prep_kernel, scale=float((MLA_NOPE + MLA_ROPE) ** -0.5)),
        grid=(t // tm,),
        in_specs=[pl.BlockSpec((tm, MLA_Q_LORA), row), pl.BlockSpec((tm, MLA_KV_LORA), row),
                  pl.BlockSpec((tm, LANES), row),
                  pl.BlockSpec((tm, wfull), pos), pl.BlockSpec((tm, wfull), pos),
                  pl.BlockSpec((1, MLA_Q_LORA), fixed), pl.BlockSpec((1, MLA_KV_LORA), fixed),
                  pl.BlockSpec((MLA_Q_LORA, wfull), fixed), pl.BlockSpec((MLA_Q_LORA, wfull), fixed),
                  pl.BlockSpec((MLA_KV_LORA, wfull), fixed), pl.BlockSpec((MLA_KV_LORA, ATT_W), fixed),
                  pl.BlockSpec((LANES, wfull), fixed), pl.BlockSpec((LANES, wfull), fixed)],
        out_specs=[pl.BlockSpec((tm, wfull), row), pl.BlockSpec((tm, wfull), row),
                   pl.BlockSpec((tm, ATT_W), row)],
        out_shape=[jax.ShapeDtypeStruct((t, wfull), BF16), jax.ShapeDtypeStruct((t, wfull), BF16),
                   jax.ShapeDtypeStruct((t, ATT_W), BF16)],
        compiler_params=_cparams(("parallel",)),
        name="mla_prep",
    )(cq, ckv, small, cos_t, sin_t, gq, gkv, wq, wqr, wk, wv, pk, pkr)


def _merge_kernel(h_ref, g_ref, ya_ref, yb_ref, yc_ref, yd_ref, wg_ref, bg_ref, wb_ref, wo_ref, o_ref):
    h = h_ref[...]
    xn = _rms(h, g_ref[...]).astype(BF16)
    merged = jnp.zeros(h.shape, F32)
    for i, y_ref in enumerate((ya_ref, yb_ref, yc_ref, yd_ref)):
        gate = _sigmoid(_dot(xn, wg_ref[i]) + bg_ref[i])
        merged = merged + gate * _dot(y_ref[...], wb_ref[i])
    o_ref[...] = h + _dot(merged.astype(BF16), wo_ref[...])


def _merge(h, g, ys, wg, bg, wb, wo):
    t, d = h.shape
    tm = min(TOKEN_TILE, t)
    row = lambda i: (i, 0)
    fixed2 = lambda i: (0, 0)
    fixed3 = lambda i: (0, 0, 0)
    return pl.pallas_call(
        _merge_kernel,
        grid=(t // tm,),
        in_specs=[pl.BlockSpec((tm, d), row), pl.BlockSpec((1, d), fixed2)]
        + [pl.BlockSpec((tm, BRANCH_W), row)] * N_BRANCH
        + [pl.BlockSpec((N_BRANCH, d, d), fixed3), pl.BlockSpec((N_BRANCH, 1, d), fixed3),
           pl.BlockSpec((N_BRANCH, BRANCH_W, d), fixed3), pl.BlockSpec((d, d), fixed2)],
        out_specs=pl.BlockSpec((tm, d), row),
        out_shape=jax.ShapeDtypeStruct((t, d), F32),
        compiler_params=_cparams(("parallel",)),
        name="merge",
    )(h, g, *ys, wg, bg, wb, wo)


def _route_kernel(h_ref, g_ref, wr_ref, br_ref, hn_ref, ri_ref, rw_ref, cnt_ref, carry_ref):
    i = pl.program_id(0)
    tm = h_ref.shape[0]
    hn = _rms(h_ref[...], g_ref[...])
    hn_ref[...] = hn
    lg = jnp.dot(hn, wr_ref[...], precision=lax.Precision.HIGHEST, preferred_element_type=F32) + br_ref[...]
    lane = lax.broadcasted_iota(I32, lg.shape, 1)
    big = jnp.int32(4 * LANES)

    def top1(vals):
        mx = jnp.max(vals, axis=1, keepdims=True)
        idx = jnp.min(jnp.where(vals == mx, lane, big), axis=1, keepdims=True)
        return mx, idx

    is_g = (lane >= N_EXPERTS) & (lane < N_EXPERTS + N_GROUPS)
    gmax, gidx = top1(jnp.where(is_g, lg, NEG_INF))
    p_sel = 1.0 / jnp.sum(jnp.where(is_g, jnp.exp(lg - gmax), 0.0), axis=1, keepdims=True)
    lo = (gidx - N_EXPERTS) * EXPERTS_PER_GROUP
    el = jnp.where((lane >= lo) & (lane < lo + EXPERTS_PER_GROUP), lg, NEG_INF)
    l1, i1 = top1(el)
    l2, i2 = top1(jnp.where(lane == i1, NEG_INF, el))
    e = jnp.exp(l2 - l1)
    w1 = p_sel / (1.0 + e)
    w2 = p_sel * e / (1.0 + e)

    @pl.when(i == 0)
    def _():
        carry_ref[...] = jnp.zeros_like(carry_ref)

    hit1, hit2 = lane == i1, lane == i2
    onehot = jnp.where(hit1 | hit2, 1.0, 0.0)
    r_i = lax.broadcasted_iota(I32, (tm, tm), 0)
    c_i = lax.broadcasted_iota(I32, (tm, tm), 1)
    before = _dot(jnp.where(c_i < r_i, 1.0, 0.0).astype(BF16), onehot.astype(BF16)) + carry_ref[...]
    rank1 = jnp.sum(jnp.where(hit1, before, 0.0), axis=1, keepdims=True).astype(I32)
    rank2 = jnp.sum(jnp.where(hit2, before, 0.0), axis=1, keepdims=True).astype(I32)
    carry_ref[...] = carry_ref[...] + jnp.sum(onehot, axis=0, keepdims=True)
    cnt_ref[...] = carry_ref[...]
    ri_ref[...] = jnp.where(lane == 0, i1, jnp.where(lane == 1, i2, jnp.where(lane == 2, rank1,
                            jnp.where(lane == 3, rank2, 0))))
    rw_ref[...] = jnp.where(lane == 0, w1, jnp.where(lane == 1, w2, 0.0))


def _route(h, g, wr, br):
    t, d = h.shape
    tm = min(TOKEN_TILE, t)
    row = lambda i: (i, 0)
    fixed = lambda i: (0, 0)
    return pl.pallas_call(
        _route_kernel,
        grid=(t // tm,),
        in_specs=[pl.BlockSpec((tm, d), row), pl.BlockSpec((1, d), fixed),
                  pl.BlockSpec((d, LANES), fixed), pl.BlockSpec((1, LANES), fixed)],
        out_specs=[pl.BlockSpec((tm, d), row), pl.BlockSpec((tm, LANES), row),
                   pl.BlockSpec((tm, LANES), row), pl.BlockSpec((1, LANES), fixed)],
        out_shape=[jax.ShapeDtypeStruct((t, d), F32), jax.ShapeDtypeStruct((t, LANES), I32),
                   jax.ShapeDtypeStruct((t, LANES), F32), jax.ShapeDtypeStruct((1, LANES), F32)],
        scratch_shapes=[pltpu.VMEM((1, LANES), F32)],
        compiler_params=_cparams(("arbitrary",)),
        name="route",
    )(h, g, wr, br)


def _ffn_kernel(blk_e_ref, blk_on_ref, x_ref, w13_ref, w2_ref, o_ref):
    i = pl.program_id(0)

    @pl.when(blk_on_ref[i] == 1)
    def _():
        x = x_ref[...].astype(BF16)
        ab = _dot(x, w13_ref[0])
        a, b = ab[:, :EXPERT_FF], ab[:, EXPERT_FF:]
        act = (a * _sigmoid(a) * b).astype(BF16)
        o_ref[...] = _dot(act, w2_ref[0])

    @pl.when(blk_on_ref[i] == 0)
    def _():
        o_ref[...] = jnp.zeros_like(o_ref)


def _expert_ffn(xs, blk_e, blk_on, w13, w2):
    p_rows, d = xs.shape
    n_blk = p_rows // FFN_ROWS
    grid_spec = pltpu.PrefetchScalarGridSpec(
        num_scalar_prefetch=2,
        grid=(n_blk,),
        in_specs=[pl.BlockSpec((FFN_ROWS, d), lambda i, be, bo: (i, 0)),
                  pl.BlockSpec((1, d, 2 * EXPERT_FF), lambda i, be, bo: (be[i], 0, 0)),
                  pl.BlockSpec((1, EXPERT_FF, d), lambda i, be, bo: (be[i], 0, 0))],
        out_specs=pl.BlockSpec((FFN_ROWS, d), lambda i, be, bo: (i, 0)),
    )
    return pl.pallas_call(
        _ffn_kernel,
        grid_spec=grid_spec,
        out_shape=jax.ShapeDtypeStruct((p_rows, d), F32),
        compiler_params=_cparams(("arbitrary",)),
        name="expert_ffn",
    )(blk_e, blk_on, xs, w13, w2)


def _final_kernel(h_ref, g_ref, o_ref):
    o_ref[...] = _rms(h_ref[...], g_ref[...])


def _final_norm(h, g):
    t, d = h.shape
    tm = min(TOKEN_TILE, t)
    return pl.pallas_call(
        _final_kernel,
        grid=(t // tm,),
        in_specs=[pl.BlockSpec((tm, d), lambda i: (i, 0)), pl.BlockSpec((1, d), lambda i: (0, 0))],
        out_specs=pl.BlockSpec((tm, d), lambda i: (i, 0)),
        out_shape=jax.ShapeDtypeStruct((t, d), F32),
        compiler_params=_cparams(("parallel",)),
        name="final_norm",
    )(h, g)


def _prep_inproj(w_in, b_forget):
    d = w_in.shape[0]
    o = np.cumsum([0, 3 * ATT_W, 3 * ATT_W, N_HEADS, 2 * CONV_CH, MLA_Q_LORA, MLA_KV_LORA, MLA_ROPE]).tolist()
    qscale = jnp.concatenate([jnp.full((ATT_W,), HEAD_DIM ** -0.5, F32), jnp.ones((2 * ATT_W,), F32)])
    parts = [w_in[:, o[0]:o[1]] * qscale, w_in[:, o[1]:o[2]] * qscale, w_in[:, o[3]:o[4]],
             w_in[:, o[4]:o[5]], w_in[:, o[5]:o[6]], w_in[:, o[6]:o[7]], w_in[:, o[2]:o[3]],
             jnp.zeros((d, C_END - C_SMALL - MLA_ROPE - N_HEADS), F32)]
    w = jnp.concatenate(parts, axis=1).astype(BF16)
    bf = jnp.zeros((1, LANES), F32).at[0, SM_CUM:SM_CUM + N_HEADS].set(b_forget)
    return w, bf


def _rope_placement():
    hp, half = MLA_HEAD_PAD, MLA_ROPE // 2
    pk = np.zeros((LANES, hp * N_HEADS), np.float32)
    pkr = np.zeros((LANES, hp * N_HEADS), np.float32)
    for h in range(N_HEADS):
        for i in range(MLA_ROPE):
            pk[SM_ROPE + i, h * hp + MLA_NOPE + i] = 1.0
        for i in range(half):
            pkr[SM_ROPE + half + i, h * hp + MLA_NOPE + i] = -1.0
            pkr[SM_ROPE + i, h * hp + MLA_NOPE + half + i] = 1.0
    return jnp.asarray(pk, BF16), jnp.asarray(pkr, BF16)


def _prep_mla(wuq, wukv):
    hp, half = MLA_HEAD_PAD, MLA_ROPE // 2
    qd = MLA_NOPE + MLA_ROPE
    zq = jnp.zeros((MLA_Q_LORA, hp - qd), F32)
    wq, wqr, wk, wv = [], [], [], []
    for h in range(N_HEADS):
        nope = wuq[:, h * qd:h * qd + MLA_NOPE]
        pe = wuq[:, h * qd + MLA_NOPE:(h + 1) * qd]
        wq += [nope, pe, zq]
        wqr += [jnp.zeros_like(nope), -pe[:, half:], pe[:, :half], zq]
        kv = wukv[:, h * (MLA_NOPE + MLA_V):(h + 1) * (MLA_NOPE + MLA_V)]
        wk += [kv[:, :MLA_NOPE], jnp.zeros((MLA_KV_LORA, hp - MLA_NOPE), F32)]
        wv += [kv[:, MLA_NOPE:]]
    cat = lambda xs: jnp.concatenate(xs, axis=1).astype(BF16)
    return cat(wq), cat(wqr), cat(wk), cat(wv)


def _rope_tables(seq):
    inv_freq = ROPE_THETA ** (-jnp.arange(0, MLA_ROPE, 2, dtype=F32) / MLA_ROPE)
    ang = jnp.arange(seq, dtype=F32)[:, None] * inv_freq[None, :]
    cos, sin = jnp.cos(ang), jnp.sin(ang)
    one = jnp.ones((seq, MLA_NOPE), F32)
    zero = jnp.zeros((seq, MLA_NOPE), F32)
    pad = jnp.zeros((seq, MLA_HEAD_PAD - MLA_NOPE - MLA_ROPE), F32)
    cos_h = jnp.concatenate([one, cos, cos, pad], axis=1)
    sin_h = jnp.concatenate([zero, sin, sin, pad], axis=1)
    return jnp.tile(cos_h, (1, N_HEADS)), jnp.tile(sin_h, (1, N_HEADS))


def kernel(x, norm1_g, w_in, b_forget, conv_w, conv_b, conv_ln_g, conv_ln_b, mla_gq, mla_wuq, mla_gkv,
           mla_wukv, w_gate, b_gate, w_branch, w_out, norm2_g, w_router_group, b_router_group,
           w_router_expert, b_router_expert, w_exp_gate, w_exp_up, w_exp_down, final_g):
    batch, seq, d = x.shape
    t = batch * seq
    depth = w_in.shape[0]
    assert seq % MOBA_BLOCK == 0 and t % TOKEN_TILE == 0
    cos_t, sin_t = _rope_tables(seq)
    pk, pkr = _rope_placement()
    n_slots = 2 * t
    p_rows = n_slots + N_EXPERTS * FFN_ROWS
    n_blk = p_rows // FFN_ROWS

    h = x.reshape(t, d)
    for l in range(depth):
        w_l, bf_l = _prep_inproj(w_in[l], b_forget[l])
        moba, fox, conv_in, cq, ckv, small, cumt = _inproj(h, norm1_g[l][None], w_l, bf_l, seq=seq)
        y_a = _attention("moba", moba, 0, moba, 1, moba, 2, batch=batch, seq=seq)
        y_b = _attention("fox", fox, 0, fox, 1, fox, 2, batch=batch, seq=seq, extra=(small, cumt))
        conv_wp = jnp.zeros((32, CONV_CH), F32).at[:CONV_K].set(conv_w[l])
        conv_p = jnp.zeros((SUBLANES, CONV_CH), F32).at[0].set(conv_b[l]).at[1].set(conv_ln_g[l]).at[2].set(
            conv_ln_b[l])
        y_c = _conv_module(conv_in, conv_wp, conv_p, batch=batch, seq=seq)
        wq, wqr, wk, wv = _prep_mla(mla_wuq[l], mla_wukv[l])
        qd, kd, vd = _mla_prep(cq, ckv, small, cos_t, sin_t, mla_gq[l][None], mla_gkv[l][None],
                               wq, wqr, wk, wv, pk, pkr, seq=seq)
        y_d = _attention("mla", qd, 0, kd, 0, vd, 0, batch=batch, seq=seq)
        h = _merge(h, norm1_g[l][None], (y_a, y_b, y_c, y_d), w_gate[l].astype(BF16), b_gate[l][:, None, :],
                   w_branch[l].astype(BF16), w_out[l].astype(BF16))

        wr = jnp.zeros((d, LANES), F32).at[:, :N_EXPERTS].set(w_router_expert[l]).at[
            :, N_EXPERTS:N_EXPERTS + N_GROUPS].set(w_router_group[l])
        br = jnp.zeros((1, LANES), F32).at[0, :N_EXPERTS].set(b_router_expert[l]).at[
            0, N_EXPERTS:N_EXPERTS + N_GROUPS].set(b_router_group[l])
        hn, ri, rw, cnt = _route(h, norm2_g[l][None], wr, br)
        counts = cnt[0, :N_EXPERTS].astype(I32)
        pcounts = ((counts + FFN_ROWS - 1) // FFN_ROWS) * FFN_ROWS
        pends = jnp.cumsum(pcounts)
        pstarts = pends - pcounts
        dest = pstarts[ri[:, 0:2]] + ri[:, 2:4]
        xs = jnp.zeros((p_rows, d), F32).at[dest.reshape(-1)].set(jnp.repeat(hn, 2, axis=0))
        blk_start = jnp.arange(n_blk, dtype=I32) * FFN_ROWS
        blk_e = jnp.minimum(jnp.searchsorted(pends, blk_start, side="right"), N_EXPERTS - 1).astype(I32)
        blk_on = (blk_start < pends[-1]).astype(I32)
        w13 = jnp.concatenate([w_exp_gate[l], w_exp_up[l]], axis=-1).astype(BF16)
        out = _expert_ffn(xs, blk_e, blk_on, w13, w_exp_down[l].astype(BF16))
        h = h + out[dest[:, 0]] * rw[:, 0:1] + out[dest[:, 1]] * rw[:, 1:2]
    return _final_norm(h, final_g[None]).reshape(batch, seq, d)
```

```python
import functools

import numpy as np
import jax
import jax.numpy as jnp
from jax import lax
from jax.experimental import pallas as pl
from jax.experimental.pallas import tpu as pltpu

F32 = jnp.float32
BF16 = jnp.bfloat16
I32 = jnp.int32

HEAD_DIM = 64
N_HEADS = 4
ATT_W = N_HEADS * HEAD_DIM
MOBA_BLOCK = 256
MOBA_TOPK = 3
CONV_CH = 256
CONV_K = 31
MLA_Q_LORA = 256
MLA_KV_LORA = 128
MLA_NOPE = 64
MLA_ROPE = 32
MLA_V = 64
MLA_HEAD_PAD = 128
ROPE_THETA = 10000.0
BRANCH_W = 256
N_BRANCH = 4
N_GROUPS = 4
EXPERTS_PER_GROUP = 8
N_EXPERTS = N_GROUPS * EXPERTS_PER_GROUP
EXPERT_FF = 256
NORM_EPS = 1e-6
NEG_INF = -1e30
LOG2E = 1.4426950408889634

LANES = 128
SUBLANES = 8
VMEM_LIMIT = 56 * 1024 * 1024

TOKEN_TILE = 512
ATT_TK = 256
V_ROWS = 80
CONV_CHUNK = 128
FFN_ROWS = 256

C_MOBA, C_FOX, C_CONV, C_CQ, C_CKV, C_SMALL, C_END = 0, 768, 1536, 2048, 2304, 2432, 2560
SM_ROPE = 0
SM_CUM = 32


def _cparams(sem):
    return pltpu.CompilerParams(dimension_semantics=sem, vmem_limit_bytes=VMEM_LIMIT)


def _rms(x, g):
    return x * lax.rsqrt(jnp.mean(x * x, axis=-1, keepdims=True) + NORM_EPS) * g


def _sigmoid(x):
    return 1.0 / (1.0 + jnp.exp(-x))


def _dot(a, b):
    return jnp.dot(a, b, preferred_element_type=F32)


def _values_t(v):
    vt = v.T
    ones = jnp.ones((V_ROWS - HEAD_DIM, vt.shape[1]), F32)
    parts = [x for h in range(N_HEADS) for x in (vt[h * HEAD_DIM:(h + 1) * HEAD_DIM, :], ones)]
    return jnp.concatenate(parts, axis=0).astype(BF16)


def _inproj_kernel(h_ref, g_ref, w_ref, bf_ref, moba_ref, mvt_ref, fox_ref, fvt_ref, conv_ref, cq_ref, ckv_ref,
                   small_ref, cumt_ref, carry_ref, *, tiles_per_seq):
    i = pl.program_id(0)
    tm = h_ref.shape[0]
    xn = _rms(h_ref[...], g_ref[...]).astype(BF16)

    def proj(a, b):
        return _dot(xn, w_ref[:, a:b])

    for qk_ref, vt_ref, c0 in ((moba_ref, mvt_ref, C_MOBA), (fox_ref, fvt_ref, C_FOX)):
        qk_ref[:, :ATT_W] = (proj(c0, c0 + ATT_W) * (HEAD_DIM ** -0.5 * LOG2E)).astype(BF16)
        qk_ref[:, ATT_W:] = proj(c0 + ATT_W, c0 + 2 * ATT_W).astype(BF16)
        vt_ref[0] = _values_t(proj(c0 + 2 * ATT_W, c0 + 3 * ATT_W))
    conv_ref[...] = proj(C_CONV, C_CQ).astype(BF16)
    cq_ref[...] = proj(C_CQ, C_CKV).astype(BF16)
    ckv_ref[...] = proj(C_CKV, C_SMALL).astype(BF16)
    sm = proj(C_SMALL, C_END)

    f = sm + bf_ref[...]
    c = jnp.minimum(f, 0.0) - jnp.log(1.0 + jnp.exp(-jnp.abs(f)))
    row = lax.broadcasted_iota(I32, c.shape, 0)
    sh = 1
    while sh < tm:
        c = c + jnp.where(row >= sh, pltpu.roll(c, sh, 0), 0.0)
        sh *= 2

    @pl.when(i % tiles_per_seq == 0)
    def _():
        carry_ref[...] = jnp.zeros_like(carry_ref)

    c = c + carry_ref[...]
    carry_ref[...] = c[tm - 1:tm, :]
    lane = lax.broadcasted_iota(I32, c.shape, 1)
    small_ref[...] = jnp.where((lane >= SM_CUM) & (lane < SM_CUM + N_HEADS), c, sm)
    cumt_ref[0] = c.T[SM_CUM:SM_CUM + SUBLANES, :]


def _inproj(h, g, w, bf, *, seq):
    t, d = h.shape
    tm = min(TOKEN_TILE, seq)
    tiles_per_seq = seq // tm
    nb = t // seq
    row = lambda i: (i, 0)
    fixed = lambda i: (0, 0)
    seq_t = lambda i: (i // tiles_per_seq, 0, i % tiles_per_seq)
    tok = lambda wdt, dt: (pl.BlockSpec((tm, wdt), row), jax.ShapeDtypeStruct((t, wdt), dt))
    chan = lambda c, dt: (pl.BlockSpec((1, c, tm), seq_t), jax.ShapeDtypeStruct((nb, c, seq), dt))
    outs = [tok(2 * ATT_W, BF16), chan(N_HEADS * V_ROWS, BF16), tok(2 * ATT_W, BF16), chan(N_HEADS * V_ROWS, BF16),
            tok(C_CQ - C_CONV, BF16), tok(C_CKV - C_CQ, BF16), tok(C_SMALL - C_CKV, BF16),
            tok(C_END - C_SMALL, F32), chan(SUBLANES, F32)]
    return pl.pallas_call(
        functools.partial(_inproj_kernel, tiles_per_seq=tiles_per_seq),
        grid=(t // tm,),
        in_specs=[pl.BlockSpec((tm, d), row), pl.BlockSpec((1, d), fixed),
                  pl.BlockSpec((d, C_END), fixed), pl.BlockSpec((1, LANES), fixed)],
        out_specs=[o[0] for o in outs],
        out_shape=[o[1] for o in outs],
        scratch_shapes=[pltpu.VMEM((1, LANES), F32)],
        compiler_params=_cparams(("arbitrary",)),
        name="inproj",
    )(h, g, w, bf)


def _attn_kernel(*refs, mode, tq, tk, nblk, n_sel):
    qs_ref, ua_ref, ub_ref, mba_ref, mbb_ref, m_ref, acc_ref = refs[-7:]
    refs = refs[:-7]
    if mode == "fox":
        q_ref, k_ref, vt_ref, small_ref, cumt_ref, o_ref, csb_ref = refs
    elif mode == "moba":
        q_ref, k_ref, vt_ref, o_ref, kmean_ref, selb_ref = refs
    else:
        q_ref, k_ref, vt_ref, o_ref = refs
    i = pl.program_id(1)
    nh = N_HEADS
    rows = nh * tq
    seq = k_ref.shape[1]
    n_full = (i * tq) // tk
    q0 = i * tq

    q = q_ref[0]
    if mode == "mla":
        for h in range(nh):
            qs_ref[:, h * tq:(h + 1) * tq] = q[:, h * MLA_HEAD_PAD:(h + 1) * MLA_HEAD_PAD].astype(F32).T.astype(BF16)
    else:
        qt = q.astype(F32).T.astype(BF16)
        sub_h = lax.shift_right_logical(lax.broadcasted_iota(I32, (ATT_W, tq), 0), 6)
        for h in range(nh):
            qs_ref[:, h * tq:(h + 1) * tq] = jnp.where(sub_h == h, qt, jnp.zeros_like(qt))

    r_in = lax.broadcasted_iota(I32, (tk, rows), 1) & (tq - 1)
    c_in = lax.broadcasted_iota(I32, (tk, rows), 0)

    def per_head_rows(x):
        return jnp.concatenate(
            [jnp.broadcast_to(x[:, h * tq:(h + 1) * tq], (V_ROWS, tq)) for h in range(nh)], axis=0)

    if mode == "fox":
        @pl.when(i == 0)
        def _():
            def fill(c, _):
                r0 = pl.multiple_of(c * tk, tk)
                cs = small_ref[0, pl.ds(r0, tk), :]
                for h in range(nh):
                    csb_ref[pl.ds(r0, tk), h * LANES:(h + 1) * LANES] = jnp.broadcast_to(
                        -LOG2E * cs[:, SM_CUM + h:SM_CUM + h + 1], (tk, LANES))
                return 0
            lax.fori_loop(0, seq // tk, fill, 0)

        ct = cumt_ref[0] * LOG2E
        c_t = jnp.concatenate([ct[h:h + 1, :] for h in range(nh)], axis=1)
    if mode == "moba":
        slopes = [LOG2E * 2.0 ** (-8.0 * (h + 1) / nh) for h in range(nh)]
        slope_row = jnp.concatenate([jnp.full((1, tq), s, F32) for s in slopes], axis=1)
        alibi_key = slope_row * c_in.astype(F32)
        r_row = (lax.broadcasted_iota(I32, (1, rows), 1) & (tq - 1)).astype(F32)
        nbp = kmean_ref.shape[0]

        @pl.when(i == 0)
        def _():
            kmean_ref[...] = jnp.zeros_like(kmean_ref)
            for n in range(nblk):
                kb = k_ref[0, n * MOBA_BLOCK:(n + 1) * MOBA_BLOCK, :].astype(F32)
                kmean_ref[n:n + 1, :] = jnp.mean(kb, axis=0, keepdims=True)

        km = kmean_ref[...]
        km_hi = km.astype(BF16)
        km_lo = (km - km_hi.astype(F32)).astype(BF16)
        cur = (i * tq) // MOBA_BLOCK
        qs = qs_ref[...]
        gate = _dot(km_hi, qs) + _dot(km_lo, qs)
        n_iota = lax.broadcasted_iota(I32, gate.shape, 0)
        gate = jnp.where(n_iota < cur, gate, NEG_INF)
        rank = jnp.zeros(gate.shape, I32)
        for n2 in range(nblk):
            gb = gate[n2:n2 + 1, :]
            ahead = jnp.where(gb > gate, 1, jnp.where((gb == gate) & (n_iota > n2), 1, 0))
            rank = rank + ahead
        selb_ref[...] = jnp.where((rank < n_sel) & (n_iota < cur), 0.0, NEG_INF)

    def produce(u_ref, mb_ref, j, diagonal=False):
        k0 = pl.multiple_of(j * tk, tk)
        kj = k_ref[0, pl.ds(k0, tk), :]
        if mode == "mla":
            u = jnp.concatenate(
                [_dot(kj[:, h * MLA_HEAD_PAD:(h + 1) * MLA_HEAD_PAD], qs_ref[:, h * tq:(h + 1) * tq])
                 for h in range(nh)], axis=1)
        else:
            u = _dot(kj, qs_ref[...])
        if mode == "fox":
            cs = csb_ref[pl.ds(k0, tk), :]
            u = u + jnp.concatenate([cs[:, h * LANES:(h + 1) * LANES] for h in range(nh)
                                     for _ in range(tq // LANES)], axis=1)
        if mode == "moba":
            u = u + alibi_key
        if diagonal:
            u = jnp.where(c_in + j * tk <= r_in + q0, u, NEG_INF)
        u_ref[...] = u
        mb_ref[...] = jnp.max(u, axis=0, keepdims=True)

    def consume(u_ref, mb_ref, j, diagonal=False):
        vtj = vt_ref[0, :, pl.ds(pl.multiple_of(j * tk, tk), tk)]
        m = m_ref[...]
        m_blk = mb_ref[...]
        if mode == "fox":
            row = c_t
        if mode == "moba":
            row = slope_row * ((j * tk - q0).astype(F32) - r_row)
            if not diagonal:
                row = row + selb_ref[pl.ds(j, 1), :]
        if mode != "mla":
            m_blk = m_blk + row
        m_new = jnp.maximum(m, m_blk)
        alpha = jnp.exp2(m - m_new)
        shift = m_new - row if mode != "mla" else m_new
        os_ = []
        for h in range(nh):
            cols = slice(h * tq, (h + 1) * tq)
            p = jnp.exp2(u_ref[:, cols] - shift[:, cols])
            os_.append(_dot(vtj[h * V_ROWS:(h + 1) * V_ROWS, :], p.astype(BF16)))
        m_ref[...] = m_new
        acc_ref[...] = per_head_rows(alpha) * acc_ref[...] + jnp.concatenate(os_, axis=0)

    m_ref[...] = jnp.full(m_ref.shape, NEG_INF, F32)
    acc_ref[...] = jnp.zeros(acc_ref.shape, F32)
    produce(ub_ref, mbb_ref, n_full, True)
    produce(ua_ref, mba_ref, 0)
    consume(ub_ref, mbb_ref, n_full, True)

    def pair(t, _):
        produce(ub_ref, mbb_ref, 2 * t + 1)
        consume(ua_ref, mba_ref, 2 * t)
        produce(ua_ref, mba_ref, 2 * t + 2)
        consume(ub_ref, mbb_ref, 2 * t + 1)
        return 0

    lax.fori_loop(0, n_full // 2, pair, 0)

    @pl.when(n_full % 2 == 1)
    def _():
        consume(ua_ref, mba_ref, n_full - 1)

    acc = acc_ref[...]
    out_t = jnp.concatenate(
        [acc[h * V_ROWS:h * V_ROWS + HEAD_DIM, :] * (1.0 / acc[h * V_ROWS + HEAD_DIM:h * V_ROWS + HEAD_DIM + 1, :])
         for h in range(nh)], axis=0)
    o_ref[0] = out_t.T.astype(BF16)


def _attention(mode, q_arr, q_blk, k_arr, k_blk, vt_arr, *, batch, seq, extra=()):
    tq = tk = min(ATT_TK, seq)
    assert tq % LANES == 0 and seq % tk == 0
    dq = MLA_HEAD_PAD * N_HEADS if mode == "mla" else ATT_W
    nblk = seq // MOBA_BLOCK
    rows = N_HEADS * tq
    r3 = lambda a: a.reshape(batch, seq, a.shape[-1])
    in_specs = [pl.BlockSpec((1, tq, dq), lambda b, i: (b, i, q_blk)),
                pl.BlockSpec((1, seq, dq), lambda b, i: (b, 0, k_blk)),
                pl.BlockSpec((1, N_HEADS * V_ROWS, seq), lambda b, i: (b, 0, 0))]
    args = [r3(q_arr), r3(k_arr), vt_arr]
    scratch = []
    if mode == "fox":
        small, cumt = extra
        in_specs += [pl.BlockSpec((1, seq, LANES), lambda b, i: (b, 0, 0)),
                     pl.BlockSpec((1, SUBLANES, tq), lambda b, i: (b, 0, i))]
        args += [r3(small), cumt]
        scratch = [pltpu.VMEM((seq, N_HEADS * LANES), F32)]
    if mode == "moba":
        nbp = -(-nblk // SUBLANES) * SUBLANES
        scratch = [pltpu.VMEM((nbp, ATT_W), F32), pltpu.VMEM((nbp, rows), F32)]
    scratch += [pltpu.VMEM((dq // N_HEADS if mode == "mla" else ATT_W, rows), BF16),
                pltpu.VMEM((tk, rows), F32), pltpu.VMEM((tk, rows), F32),
                pltpu.VMEM((1, rows), F32), pltpu.VMEM((1, rows), F32),
                pltpu.VMEM((1, rows), F32),
                pltpu.VMEM((N_HEADS * V_ROWS, tq), F32)]
    out = pl.pallas_call(
        functools.partial(_attn_kernel, mode=mode, tq=tq, tk=tk, nblk=nblk, n_sel=min(MOBA_TOPK, nblk - 1)),
        grid=(batch, seq // tq),
        in_specs=in_specs,
        out_specs=pl.BlockSpec((1, tq, ATT_W), lambda b, i: (b, i, 0)),
        out_shape=jax.ShapeDtypeStruct((batch, seq, ATT_W), BF16),
        scratch_shapes=scratch,
        compiler_params=_cparams(("parallel", "arbitrary")),
        name="attn_" + mode,
    )(*args)
    return out.reshape(batch * seq, ATT_W)


def _conv_kernel(x_ref, w_ref, p_ref, o_ref, u_ref, *, chunk):
    seq = x_ref.shape[1]
    halo = 32
    u_ref[0:halo, :] = jnp.zeros((halo, CONV_CH), F32)

    def glu(c, _):
        r0 = pl.multiple_of(c * chunk, chunk)
        x = x_ref[0, pl.ds(r0, chunk), :].astype(F32)
        u_ref[pl.ds(halo + r0, chunk), :] = x[:, :CONV_CH] * _sigmoid(x[:, CONV_CH:])
        return 0

    lax.fori_loop(0, seq // chunk, glu, 0)
    bias, ln_g, ln_b = p_ref[0:1, :], p_ref[1:2, :], p_ref[2:3, :]

    def conv(c, _):
        r0 = pl.multiple_of(c * chunk, chunk)
        n = chunk + halo
        win = u_ref[pl.ds(r0, n), :]
        acc = jnp.zeros((chunk, CONV_CH), F32)
        for b in range(SUBLANES):
            shifted = win if b == 0 else pltpu.roll(win, n - b, 0)
            for a in range(halo // SUBLANES + 1):
                k = SUBLANES * a + b - (halo - (CONV_K - 1))
                if 0 <= k < CONV_K:
                    acc = acc + w_ref[k:k + 1, :] * shifted[SUBLANES * a:SUBLANES * a + chunk, :]
        acc = acc + bias
        mu = jnp.mean(acc, axis=-1, keepdims=True)
        xc = acc - mu
        var = jnp.mean(xc * xc, axis=-1, keepdims=True)
        y = xc * lax.rsqrt(var + NORM_EPS) * ln_g + ln_b
        o_ref[0, pl.ds(r0, chunk), :] = (y * _sigmoid(y)).astype(BF16)
        return 0

    lax.fori_loop(0, seq // chunk, conv, 0)


def _conv_module(conv_in, w, p, *, batch, seq):
    chunk = min(CONV_CHUNK, seq)
    out = pl.pallas_call(
        functools.partial(_conv_kernel, chunk=chunk),
        grid=(batch,),
        in_specs=[pl.BlockSpec((1, seq, 2 * CONV_CH), lambda b: (b, 0, 0)),
                  pl.BlockSpec((32, CONV_CH), lambda b: (0, 0)),
                  pl.BlockSpec((SUBLANES, CONV_CH), lambda b: (0, 0))],
        out_specs=pl.BlockSpec((1, seq, CONV_CH), lambda b: (b, 0, 0)),
        out_shape=jax.ShapeDtypeStruct((batch, seq, CONV_CH), BF16),
        scratch_shapes=[pltpu.VMEM((seq + 32, CONV_CH), F32)],
        compiler_params=_cparams(("parallel",)),
        name="conv",
    )(conv_in.reshape(batch, seq, 2 * CONV_CH), w, p)
    return out.reshape(batch * seq, CONV_CH)


def _mla_prep_kernel(cq_ref, ckv_ref, small_ref, cos_ref, sin_ref, gq_ref, gkv_ref, wq_ref, wqr_ref,
                     wk_ref, wv_ref, pk_ref, pkr_ref, q_ref, k_ref, v_ref, *, scale):
    cos, sin = cos_ref[...], sin_ref[...]
    cq = _rms(cq_ref[...].astype(F32), gq_ref[...]).astype(BF16)
    q = _dot(cq, wq_ref[...]) * cos + _dot(cq, wqr_ref[...]) * sin
    q_ref[...] = (q * scale).astype(BF16)
    ckv = _rms(ckv_ref[...].astype(F32), gkv_ref[...]).astype(BF16)
    kr = small_ref[...].astype(BF16)
    k = _dot(ckv, wk_ref[...]) + _dot(kr, pk_ref[...]) * cos + _dot(kr, pkr_ref[...]) * sin
    k_ref[...] = k.astype(BF16)
    v_ref[0] = _values_t(_dot(ckv, wv_ref[...]))


def _mla_prep(cq, ckv, small, cos_t, sin_t, gq, gkv, wq, wqr, wk, wv, pk, pkr, *, seq):
    t = cq.shape[0]
    tm = min(TOKEN_TILE, seq)
    tiles_per_seq = seq // tm
    wfull = MLA_HEAD_PAD * N_HEADS
    row = lambda i: (i, 0)
    pos = lambda i: (i % tiles_per_seq, 0)
    fixed = lambda i: (0, 0)
    return pl.pallas_call(
        functools.partial(_mla_prep_kernel, scale=float((MLA_NOPE + MLA_ROPE) ** -0.5 * LOG2E)),
        grid=(t // tm,),
        in_specs=[pl.BlockSpec((tm, MLA_Q_LORA), row), pl.BlockSpec((tm, MLA_KV_LORA), row),
                  pl.BlockSpec((tm, LANES), row),
                  pl.BlockSpec((tm, wfull), pos), pl.BlockSpec((tm, wfull), pos),
                  pl.BlockSpec((1, MLA_Q_LORA), fixed), pl.BlockSpec((1, MLA_KV_LORA), fixed),
                  pl.BlockSpec((MLA_Q_LORA, wfull), fixed), pl.BlockSpec((MLA_Q_LORA, wfull), fixed),
                  pl.BlockSpec((MLA_KV_LORA, wfull), fixed), pl.BlockSpec((MLA_KV_LORA, ATT_W), fixed),
                  pl.BlockSpec((LANES, wfull), fixed), pl.BlockSpec((LANES, wfull), fixed)],
        out_specs=[pl.BlockSpec((tm, wfull), row), pl.BlockSpec((tm, wfull), row),
                   pl.BlockSpec((1, N_HEADS * V_ROWS, tm), lambda i: (i // tiles_per_seq, 0, i % tiles_per_seq))],
        out_shape=[jax.ShapeDtypeStruct((t, wfull), BF16), jax.ShapeDtypeStruct((t, wfull), BF16),
                   jax.ShapeDtypeStruct((t // seq, N_HEADS * V_ROWS, seq), BF16)],
        compiler_params=_cparams(("parallel",)),
        name="mla_prep",
    )(cq, ckv, small, cos_t, sin_t, gq, gkv, wq, wqr, wk, wv, pk, pkr)


def _merge_kernel(h_ref, g_ref, ya_ref, yb_ref, yc_ref, yd_ref, wg_ref, bg_ref, wb_ref, wo_ref, o_ref):
    h = h_ref[...]
    xn = _rms(h, g_ref[...]).astype(BF16)
    merged = jnp.zeros(h.shape, F32)
    for i, y_ref in enumerate((ya_ref, yb_ref, yc_ref, yd_ref)):
        gate = _sigmoid(_dot(xn, wg_ref[i]) + bg_ref[i])
        merged = merged + gate * _dot(y_ref[...], wb_ref[i])
    o_ref[...] = h + _dot(merged.astype(BF16), wo_ref[...])


def _merge(h, g, ys, wg, bg, wb, wo):
    t, d = h.shape
    tm = min(TOKEN_TILE, t)
    row = lambda i: (i, 0)
    fixed2 = lambda i: (0, 0)
    fixed3 = lambda i: (0, 0, 0)
    return pl.pallas_call(
        _merge_kernel,
        grid=(t // tm,),
        in_specs=[pl.BlockSpec((tm, d), row), pl.BlockSpec((1, d), fixed2)]
        + [pl.BlockSpec((tm, BRANCH_W), row)] * N_BRANCH
        + [pl.BlockSpec((N_BRANCH, d, d), fixed3), pl.BlockSpec((N_BRANCH, 1, d), fixed3),
           pl.BlockSpec((N_BRANCH, BRANCH_W, d), fixed3), pl.BlockSpec((d, d), fixed2)],
        out_specs=pl.BlockSpec((tm, d), row),
        out_shape=jax.ShapeDtypeStruct((t, d), F32),
        compiler_params=_cparams(("parallel",)),
        name="merge",
    )(h, g, *ys, wg, bg, wb, wo)


def _route_kernel(h_ref, g_ref, wr_ref, br_ref, hn_ref, ri_ref, rw_ref, cnt_ref, carry_ref):
    i = pl.program_id(0)
    tm = h_ref.shape[0]
    hn = _rms(h_ref[...], g_ref[...])
    hn_ref[...] = hn
    lg = jnp.dot(hn, wr_ref[...], precision=lax.Precision.HIGHEST, preferred_element_type=F32) + br_ref[...]
    lane = lax.broadcasted_iota(I32, lg.shape, 1)
    big = jnp.int32(4 * LANES)

    def top1(vals):
        mx = jnp.max(vals, axis=1, keepdims=True)
        idx = jnp.min(jnp.where(vals == mx, lane, big), axis=1, keepdims=True)
        return mx, idx

    is_g = (lane >= N_EXPERTS) & (lane < N_EXPERTS + N_GROUPS)
    gmax, gidx = top1(jnp.where(is_g, lg, NEG_INF))
    p_sel = 1.0 / jnp.sum(jnp.where(is_g, jnp.exp(lg - gmax), 0.0), axis=1, keepdims=True)
    lo = (gidx - N_EXPERTS) * EXPERTS_PER_GROUP
    el = jnp.where((lane >= lo) & (lane < lo + EXPERTS_PER_GROUP), lg, NEG_INF)
    l1, i1 = top1(el)
    l2, i2 = top1(jnp.where(lane == i1, NEG_INF, el))
    e = jnp.exp(l2 - l1)
    w1 = p_sel / (1.0 + e)
    w2 = p_sel * e / (1.0 + e)

    @pl.when(i == 0)
    def _():
        carry_ref[...] = jnp.zeros_like(carry_ref)

    hit1, hit2 = lane == i1, lane == i2
    onehot = jnp.where(hit1 | hit2, 1.0, 0.0)
    r_i = lax.broadcasted_iota(I32, (tm, tm), 0)
    c_i = lax.broadcasted_iota(I32, (tm, tm), 1)
    before = _dot(jnp.where(c_i < r_i, 1.0, 0.0).astype(BF16), onehot.astype(BF16)) + carry_ref[...]
    rank1 = jnp.sum(jnp.where(hit1, before, 0.0), axis=1, keepdims=True).astype(I32)
    rank2 = jnp.sum(jnp.where(hit2, before, 0.0), axis=1, keepdims=True).astype(I32)
    carry_ref[...] = carry_ref[...] + jnp.sum(onehot, axis=0, keepdims=True)
    cnt_ref[...] = carry_ref[...]
    ri_ref[...] = jnp.where(lane == 0, i1, jnp.where(lane == 1, i2, jnp.where(lane == 2, rank1,
                            jnp.where(lane == 3, rank2, 0))))
    rw_ref[...] = jnp.where(lane == 0, w1, jnp.where(lane == 1, w2, 0.0))


def _route(h, g, wr, br):
    t, d = h.shape
    tm = min(TOKEN_TILE, t)
    row = lambda i: (i, 0)
    fixed = lambda i: (0, 0)
    return pl.pallas_call(
        _route_kernel,
        grid=(t // tm,),
        in_specs=[pl.BlockSpec((tm, d), row), pl.BlockSpec((1, d), fixed),
                  pl.BlockSpec((d, LANES), fixed), pl.BlockSpec((1, LANES), fixed)],
        out_specs=[pl.BlockSpec((tm, d), row), pl.BlockSpec((tm, LANES), row),
                   pl.BlockSpec((tm, LANES), row), pl.BlockSpec((1, LANES), fixed)],
        out_shape=[jax.ShapeDtypeStruct((t, d), F32), jax.ShapeDtypeStruct((t, LANES), I32),
                   jax.ShapeDtypeStruct((t, LANES), F32), jax.ShapeDtypeStruct((1, LANES), F32)],
        scratch_shapes=[pltpu.VMEM((1, LANES), F32)],
        compiler_params=_cparams(("arbitrary",)),
        name="route",
    )(h, g, wr, br)


def _ffn_kernel(blk_e_ref, blk_on_ref, x_ref, w13_ref, w2_ref, o_ref):
    i = pl.program_id(0)

    @pl.when(blk_on_ref[i] == 1)
    def _():
        x = x_ref[...].astype(BF16)
        ab = _dot(x, w13_ref[0])
        a, b = ab[:, :EXPERT_FF], ab[:, EXPERT_FF:]
        act = (a * _sigmoid(a) * b).astype(BF16)
        o_ref[...] = _dot(act, w2_ref[0])

    @pl.when(blk_on_ref[i] == 0)
    def _():
        o_ref[...] = jnp.zeros_like(o_ref)


def _expert_ffn(xs, blk_e, blk_on, w13, w2):
    p_rows, d = xs.shape
    n_blk = p_rows // FFN_ROWS
    grid_spec = pltpu.PrefetchScalarGridSpec(
        num_scalar_prefetch=2,
        grid=(n_blk,),
        in_specs=[pl.BlockSpec((FFN_ROWS, d), lambda i, be, bo: (i, 0)),
                  pl.BlockSpec((1, d, 2 * EXPERT_FF), lambda i, be, bo: (be[i], 0, 0)),
                  pl.BlockSpec((1, EXPERT_FF, d), lambda i, be, bo: (be[i], 0, 0))],
        out_specs=pl.BlockSpec((FFN_ROWS, d), lambda i, be, bo: (i, 0)),
    )
    return pl.pallas_call(
        _ffn_kernel,
        grid_spec=grid_spec,
        out_shape=jax.ShapeDtypeStruct((p_rows, d), F32),
        compiler_params=_cparams(("arbitrary",)),
        name="expert_ffn",
    )(blk_e, blk_on, xs, w13, w2)


def _final_kernel(h_ref, g_ref, o_ref):
    o_ref[...] = _rms(h_ref[...], g_ref[...])


def _final_norm(h, g):
    t, d = h.shape
    tm = min(TOKEN_TILE, t)
    return pl.pallas_call(
        _final_kernel,
        grid=(t // tm,),
        in_specs=[pl.BlockSpec((tm, d), lambda i: (i, 0)), pl.BlockSpec((1, d), lambda i: (0, 0))],
        out_specs=pl.BlockSpec((tm, d), lambda i: (i, 0)),
        out_shape=jax.ShapeDtypeStruct((t, d), F32),
        compiler_params=_cparams(("parallel",)),
        name="final_norm",
    )(h, g)


def _prep_inproj(w_in, b_forget):
    d = w_in.shape[0]
    o = np.cumsum([0, 3 * ATT_W, 3 * ATT_W, N_HEADS, 2 * CONV_CH, MLA_Q_LORA, MLA_KV_LORA, MLA_ROPE]).tolist()
    parts = [w_in[:, o[0]:o[1]], w_in[:, o[1]:o[2]], w_in[:, o[3]:o[4]],
             w_in[:, o[4]:o[5]], w_in[:, o[5]:o[6]], w_in[:, o[6]:o[7]], w_in[:, o[2]:o[3]],
             jnp.zeros((d, C_END - C_SMALL - MLA_ROPE - N_HEADS), F32)]
    w = jnp.concatenate(parts, axis=1).astype(BF16)
    bf = jnp.zeros((1, LANES), F32).at[0, SM_CUM:SM_CUM + N_HEADS].set(b_forget)
    return w, bf


def _rope_placement():
    hp, half = MLA_HEAD_PAD, MLA_ROPE // 2
    pk = np.zeros((LANES, hp * N_HEADS), np.float32)
    pkr = np.zeros((LANES, hp * N_HEADS), np.float32)
    for h in range(N_HEADS):
        for i in range(MLA_ROPE):
            pk[SM_ROPE + i, h * hp + MLA_NOPE + i] = 1.0
        for i in range(half):
            pkr[SM_ROPE + half + i, h * hp + MLA_NOPE + i] = -1.0
            pkr[SM_ROPE + i, h * hp + MLA_NOPE + half + i] = 1.0
    return jnp.asarray(pk, BF16), jnp.asarray(pkr, BF16)


def _prep_mla(wuq, wukv):
    hp, half = MLA_HEAD_PAD, MLA_ROPE // 2
    qd = MLA_NOPE + MLA_ROPE
    zq = jnp.zeros((MLA_Q_LORA, hp - qd), F32)
    wq, wqr, wk, wv = [], [], [], []
    for h in range(N_HEADS):
        nope = wuq[:, h * qd:h * qd + MLA_NOPE]
        pe = wuq[:, h * qd + MLA_NOPE:(h + 1) * qd]
        wq += [nope, pe, zq]
        wqr += [jnp.zeros_like(nope), -pe[:, half:], pe[:, :half], zq]
        kv = wukv[:, h * (MLA_NOPE + MLA_V):(h + 1) * (MLA_NOPE + MLA_V)]
        wk += [kv[:, :MLA_NOPE], jnp.zeros((MLA_KV_LORA, hp - MLA_NOPE), F32)]
        wv += [kv[:, MLA_NOPE:]]
    cat = lambda xs: jnp.concatenate(xs, axis=1).astype(BF16)
    return cat(wq), cat(wqr), cat(wk), cat(wv)


def _rope_tables(seq):
    inv_freq = ROPE_THETA ** (-jnp.arange(0, MLA_ROPE, 2, dtype=F32) / MLA_ROPE)
    ang = jnp.arange(seq, dtype=F32)[:, None] * inv_freq[None, :]
    cos, sin = jnp.cos(ang), jnp.sin(ang)
    one = jnp.ones((seq, MLA_NOPE), F32)
    zero = jnp.zeros((seq, MLA_NOPE), F32)
    pad = jnp.zeros((seq, MLA_HEAD_PAD - MLA_NOPE - MLA_ROPE), F32)
    cos_h = jnp.concatenate([one, cos, cos, pad], axis=1)
    sin_h = jnp.concatenate([zero, sin, sin, pad], axis=1)
    return jnp.tile(cos_h, (1, N_HEADS)), jnp.tile(sin_h, (1, N_HEADS))


def kernel(x, norm1_g, w_in, b_forget, conv_w, conv_b, conv_ln_g, conv_ln_b, mla_gq, mla_wuq, mla_gkv,
           mla_wukv, w_gate, b_gate, w_branch, w_out, norm2_g, w_router_group, b_router_group,
           w_router_expert, b_router_expert, w_exp_gate, w_exp_up, w_exp_down, final_g):
    batch, seq, d = x.shape
    t = batch * seq
    depth = w_in.shape[0]
    assert seq % MOBA_BLOCK == 0 and t % TOKEN_TILE == 0
    cos_t, sin_t = _rope_tables(seq)
    pk, pkr = _rope_placement()
    n_slots = 2 * t
    p_rows = n_slots + N_EXPERTS * FFN_ROWS
    n_blk = p_rows // FFN_ROWS

    h = x.reshape(t, d)
    for l in range(depth):
        w_l, bf_l = _prep_inproj(w_in[l], b_forget[l])
        moba, moba_vt, fox, fox_vt, conv_in, cq, ckv, small, cumt = _inproj(h, norm1_g[l][None], w_l, bf_l, seq=seq)
        y_a = _attention("moba", moba, 0, moba, 1, moba_vt, batch=batch, seq=seq)
        y_b = _attention("fox", fox, 0, fox, 1, fox_vt, batch=batch, seq=seq, extra=(small, cumt))
        conv_wp = jnp.zeros((32, CONV_CH), F32).at[:CONV_K].set(conv_w[l])
        conv_p = jnp.zeros((SUBLANES, CONV_CH), F32).at[0].set(conv_b[l]).at[1].set(conv_ln_g[l]).at[2].set(
            conv_ln_b[l])
        y_c = _conv_module(conv_in, conv_wp, conv_p, batch=batch, seq=seq)
        wq, wqr, wk, wv = _prep_mla(mla_wuq[l], mla_wukv[l])
        qd, kd, vd = _mla_prep(cq, ckv, small, cos_t, sin_t, mla_gq[l][None], mla_gkv[l][None],
                               wq, wqr, wk, wv, pk, pkr, seq=seq)
        y_d = _attention("mla", qd, 0, kd, 0, vd, batch=batch, seq=seq)
        h = _merge(h, norm1_g[l][None], (y_a, y_b, y_c, y_d), w_gate[l].astype(BF16), b_gate[l][:, None, :],
                   w_branch[l].astype(BF16), w_out[l].astype(BF16))

        wr = jnp.zeros((d, LANES), F32).at[:, :N_EXPERTS].set(w_router_expert[l]).at[
            :, N_EXPERTS:N_EXPERTS + N_GROUPS].set(w_router_group[l])
        br = jnp.zeros((1, LANES), F32).at[0, :N_EXPERTS].set(b_router_expert[l]).at[
            0, N_EXPERTS:N_EXPERTS + N_GROUPS].set(b_router_group[l])
        hn, ri, rw, cnt = _route(h, norm2_g[l][None], wr, br)
        counts = cnt[0, :N_EXPERTS].astype(I32)
        pcounts = ((counts + FFN_ROWS - 1) // FFN_ROWS) * FFN_ROWS
        pends = jnp.cumsum(pcounts)
        pstarts = pends - pcounts
        dest = pstarts[ri[:, 0:2]] + ri[:, 2:4]
        xs = jnp.zeros((p_rows, d), F32).at[dest.reshape(-1)].set(jnp.repeat(hn, 2, axis=0))
        blk_start = jnp.arange(n_blk, dtype=I32) * FFN_ROWS
        blk_e = jnp.minimum(jnp.searchsorted(pends, blk_start, side="right"), N_EXPERTS - 1).astype(I32)
        blk_on = (blk_start < pends[-1]).astype(I32)
        w13 = jnp.concatenate([w_exp_gate[l], w_exp_up[l]], axis=-1).astype(BF16)
        out = _expert_ffn(xs, blk_e, blk_on, w13, w_exp_down[l].astype(BF16))
        h = h + out[dest[:, 0]] * rw[:, 0:1] + out[dest[:, 1]] * rw[:, 1:2]
    return _final_norm(h, final_g[None]).reshape(batch, seq, d)
```

```python
import functools

import numpy as np
import jax
import jax.numpy as jnp
from jax import lax
from jax.experimental import pallas as pl
from jax.experimental.pallas import tpu as pltpu

F32 = jnp.float32
BF16 = jnp.bfloat16
I32 = jnp.int32

HEAD_DIM = 64
N_HEADS = 4
ATT_W = N_HEADS * HEAD_DIM
MOBA_BLOCK = 256
MOBA_TOPK = 3
CONV_CH = 256
CONV_K = 31
MLA_Q_LORA = 256
MLA_KV_LORA = 128
MLA_NOPE = 64
MLA_ROPE = 32
MLA_V = 64
MLA_HEAD_PAD = 128
ROPE_THETA = 10000.0
BRANCH_W = 256
N_BRANCH = 4
N_GROUPS = 4
EXPERTS_PER_GROUP = 8
N_EXPERTS = N_GROUPS * EXPERTS_PER_GROUP
EXPERT_FF = 256
NORM_EPS = 1e-6
NEG_INF = -1e30
LOG2E = 1.4426950408889634

LANES = 128
SUBLANES = 8
VMEM_LIMIT = 56 * 1024 * 1024

TOKEN_TILE = 512
ATT_TK = 256
V_ROWS = 80
CONV_CHUNK = 128
FFN_ROWS = 256

C_MOBA, C_FOX, C_CONV, C_CQ, C_CKV, C_SMALL, C_END = 0, 768, 1536, 2048, 2304, 2432, 2560
SM_ROPE = 0
SM_CUM = 32


def _cparams(sem):
    return pltpu.CompilerParams(dimension_semantics=sem, vmem_limit_bytes=VMEM_LIMIT)


def _rms(x, g):
    return x * lax.rsqrt(jnp.mean(x * x, axis=-1, keepdims=True) + NORM_EPS) * g


def _sigmoid(x):
    return 1.0 / (1.0 + jnp.exp(-x))


def _dot(a, b):
    return jnp.dot(a, b, preferred_element_type=F32)


def _values_t(v):
    vt = v.T
    ones = jnp.ones((V_ROWS - HEAD_DIM, vt.shape[1]), F32)
    parts = [x for h in range(N_HEADS) for x in (vt[h * HEAD_DIM:(h + 1) * HEAD_DIM, :], ones)]
    return jnp.concatenate(parts, axis=0).astype(BF16)


def _moe_combine(h_ref, moe_refs):
    ya_ref, yb_ref, rw_ref = moe_refs
    rw = rw_ref[...]
    return h_ref[...] + rw[:, 0:1] * ya_ref[...] + rw[:, 1:2] * yb_ref[...]


def _inproj_kernel(*refs, tiles_per_seq, combine):
    if combine:
        h_ref, *moe_refs, g_ref, w_ref, bf_ref, hout_ref = refs[:8]
        refs = refs[8:]
        h = _moe_combine(h_ref, moe_refs)
        hout_ref[...] = h
    else:
        h_ref, g_ref, w_ref, bf_ref = refs[:4]
        refs = refs[4:]
        h = h_ref[...]
    moba_ref, mvt_ref, fox_ref, fvt_ref, conv_ref, cq_ref, ckv_ref, small_ref, cumt_ref, carry_ref = refs
    i = pl.program_id(0)
    tm = h_ref.shape[0]
    xn = _rms(h, g_ref[...]).astype(BF16)

    def proj(a, b):
        return _dot(xn, w_ref[:, a:b])

    for qk_ref, vt_ref, c0 in ((moba_ref, mvt_ref, C_MOBA), (fox_ref, fvt_ref, C_FOX)):
        qk_ref[:, :ATT_W] = (proj(c0, c0 + ATT_W) * (HEAD_DIM ** -0.5 * LOG2E)).astype(BF16)
        qk_ref[:, ATT_W:] = proj(c0 + ATT_W, c0 + 2 * ATT_W).astype(BF16)
        vt_ref[0] = _values_t(proj(c0 + 2 * ATT_W, c0 + 3 * ATT_W))
    conv_ref[...] = proj(C_CONV, C_CQ).astype(BF16)
    cq_ref[...] = proj(C_CQ, C_CKV).astype(BF16)
    ckv_ref[...] = proj(C_CKV, C_SMALL).astype(BF16)
    sm = proj(C_SMALL, C_END)

    f = sm + bf_ref[...]
    c = jnp.minimum(f, 0.0) - jnp.log(1.0 + jnp.exp(-jnp.abs(f)))
    row = lax.broadcasted_iota(I32, c.shape, 0)
    sh = 1
    while sh < tm:
        c = c + jnp.where(row >= sh, pltpu.roll(c, sh, 0), 0.0)
        sh *= 2

    @pl.when(i % tiles_per_seq == 0)
    def _():
        carry_ref[...] = jnp.zeros_like(carry_ref)

    c = c + carry_ref[...]
    carry_ref[...] = c[tm - 1:tm, :]
    lane = lax.broadcasted_iota(I32, c.shape, 1)
    small_ref[...] = jnp.where((lane >= SM_CUM) & (lane < SM_CUM + N_HEADS), c, sm)
    cumt_ref[0] = c.T[SM_CUM:SM_CUM + SUBLANES, :]


def _moe_specs(t, tm, d):
    return [pl.BlockSpec((tm, d), lambda i: (i, 0)), pl.BlockSpec((tm, d), lambda i: (i + t // tm, 0)),
            pl.BlockSpec((tm, LANES), lambda i: (i, 0))]


def _inproj(h, g, w, bf, *, seq, moe=None):
    t, d = h.shape
    tm = min(TOKEN_TILE, seq)
    tiles_per_seq = seq // tm
    nb = t // seq
    row = lambda i: (i, 0)
    fixed = lambda i: (0, 0)
    seq_t = lambda i: (i // tiles_per_seq, 0, i % tiles_per_seq)
    tok = lambda wdt, dt: (pl.BlockSpec((tm, wdt), row), jax.ShapeDtypeStruct((t, wdt), dt))
    chan = lambda c, dt: (pl.BlockSpec((1, c, tm), seq_t), jax.ShapeDtypeStruct((nb, c, seq), dt))
    outs = [tok(2 * ATT_W, BF16), chan(N_HEADS * V_ROWS, BF16), tok(2 * ATT_W, BF16), chan(N_HEADS * V_ROWS, BF16),
            tok(C_CQ - C_CONV, BF16), tok(C_CKV - C_CQ, BF16), tok(C_SMALL - C_CKV, BF16),
            tok(C_END - C_SMALL, F32), chan(SUBLANES, F32)]
    if moe is not None:
        outs = [tok(d, F32)] + outs
    return pl.pallas_call(
        functools.partial(_inproj_kernel, tiles_per_seq=tiles_per_seq, combine=moe is not None),
        grid=(t // tm,),
        in_specs=[pl.BlockSpec((tm, d), row)] + (_moe_specs(t, tm, d) if moe is not None else [])
        + [pl.BlockSpec((1, d), fixed), pl.BlockSpec((d, C_END), fixed), pl.BlockSpec((1, LANES), fixed)],
        out_specs=[o[0] for o in outs],
        out_shape=[o[1] for o in outs],
        scratch_shapes=[pltpu.VMEM((1, LANES), F32)],
        compiler_params=_cparams(("arbitrary",)),
        name="inproj",
    )(h, *((moe[0], moe[0], moe[1]) if moe is not None else ()), g, w, bf)


def _attn_kernel(*refs, mode, tq, tk, nblk, n_sel):
    qs_ref, ua_ref, ub_ref, mba_ref, mbb_ref, m_ref, acc_ref = refs[-7:]
    refs = refs[:-7]
    if mode == "fox":
        q_ref, k_ref, vt_ref, small_ref, cumt_ref, o_ref, csb_ref = refs
    elif mode == "moba":
        q_ref, k_ref, vt_ref, o_ref, kmean_ref, selb_ref = refs
    else:
        q_ref, k_ref, vt_ref, o_ref = refs
    i = pl.program_id(1)
    nh = N_HEADS
    rows = nh * tq
    seq = k_ref.shape[1]
    n_full = (i * tq) // tk
    q0 = i * tq

    q = q_ref[0]
    if mode == "mla":
        for h in range(nh):
            qs_ref[:, h * tq:(h + 1) * tq] = q[:, h * MLA_HEAD_PAD:(h + 1) * MLA_HEAD_PAD].astype(F32).T.astype(BF16)
    else:
        qt = q.astype(F32).T.astype(BF16)
        sub_h = lax.shift_right_logical(lax.broadcasted_iota(I32, (ATT_W, tq), 0), 6)
        for h in range(nh):
            qs_ref[:, h * tq:(h + 1) * tq] = jnp.where(sub_h == h, qt, jnp.zeros_like(qt))

    r_in = lax.broadcasted_iota(I32, (tk, rows), 1) & (tq - 1)
    c_in = lax.broadcasted_iota(I32, (tk, rows), 0)

    def per_head_rows(x):
        return jnp.concatenate(
            [jnp.broadcast_to(x[:, h * tq:(h + 1) * tq], (V_ROWS, tq)) for h in range(nh)], axis=0)

    if mode == "fox":
        @pl.when(i == 0)
        def _():
            def fill(c, _):
                r0 = pl.multiple_of(c * tk, tk)
                cs = small_ref[0, pl.ds(r0, tk), :]
                for h in range(nh):
                    csb_ref[pl.ds(r0, tk), h * LANES:(h + 1) * LANES] = jnp.broadcast_to(
                        -LOG2E * cs[:, SM_CUM + h:SM_CUM + h + 1], (tk, LANES))
                return 0
            lax.fori_loop(0, seq // tk, fill, 0)

        ct = cumt_ref[0] * LOG2E
        c_t = jnp.concatenate([ct[h:h + 1, :] for h in range(nh)], axis=1)
    if mode == "moba":
        slopes = [LOG2E * 2.0 ** (-8.0 * (h + 1) / nh) for h in range(nh)]
        slope_row = jnp.concatenate([jnp.full((1, tq), s, F32) for s in slopes], axis=1)
        alibi_key = slope_row * c_in.astype(F32)
        r_row = (lax.broadcasted_iota(I32, (1, rows), 1) & (tq - 1)).astype(F32)
        nbp = kmean_ref.shape[0]

        @pl.when(i == 0)
        def _():
            kmean_ref[...] = jnp.zeros_like(kmean_ref)
            for n in range(nblk):
                kb = k_ref[0, n * MOBA_BLOCK:(n + 1) * MOBA_BLOCK, :].astype(F32)
                kmean_ref[n:n + 1, :] = jnp.mean(kb, axis=0, keepdims=True)

        km = kmean_ref[...]
        km_hi = km.astype(BF16)
        km_lo = (km - km_hi.astype(F32)).astype(BF16)
        cur = (i * tq) // MOBA_BLOCK
        qs = qs_ref[...]
        gate = _dot(km_hi, qs) + _dot(km_lo, qs)
        n_iota = lax.broadcasted_iota(I32, gate.shape, 0)
        gate = jnp.where(n_iota < cur, gate, NEG_INF)
        rank = jnp.zeros(gate.shape, I32)
        for n2 in range(nblk):
            gb = gate[n2:n2 + 1, :]
            ahead = jnp.where(gb > gate, 1, jnp.where((gb == gate) & (n_iota > n2), 1, 0))
            rank = rank + ahead
        selb_ref[...] = jnp.where((rank < n_sel) & (n_iota < cur), 0.0, NEG_INF)

    def produce(u_ref, mb_ref, j, diagonal=False):
        k0 = pl.multiple_of(j * tk, tk)
        kj = k_ref[0, pl.ds(k0, tk), :]
        if mode == "mla":
            u = jnp.concatenate(
                [_dot(kj[:, h * MLA_HEAD_PAD:(h + 1) * MLA_HEAD_PAD], qs_ref[:, h * tq:(h + 1) * tq])
                 for h in range(nh)], axis=1)
        else:
            u = _dot(kj, qs_ref[...])
        if mode == "fox":
            cs = csb_ref[pl.ds(k0, tk), :]
            u = u + jnp.concatenate([cs[:, h * LANES:(h + 1) * LANES] for h in range(nh)
                                     for _ in range(tq // LANES)], axis=1)
        if mode == "moba":
            u = u + alibi_key
        if diagonal:
            u = jnp.where(c_in + j * tk <= r_in + q0, u, NEG_INF)
        u_ref[...] = u
        mb_ref[...] = jnp.max(u, axis=0, keepdims=True)

    def consume(u_ref, mb_ref, j, diagonal=False):
        vtj = vt_ref[0, :, pl.ds(pl.multiple_of(j * tk, tk), tk)]
        m = m_ref[...]
        m_blk = mb_ref[...]
        if mode == "fox":
            row = c_t
        if mode == "moba":
            row = slope_row * ((j * tk - q0).astype(F32) - r_row)
            if not diagonal:
                row = row + selb_ref[pl.ds(j, 1), :]
        if mode != "mla":
            m_blk = m_blk + row
        m_new = jnp.maximum(m, m_blk)
        alpha = jnp.exp2(m - m_new)
        shift = m_new - row if mode != "mla" else m_new
        os_ = []
        for h in range(nh):
            cols = slice(h * tq, (h + 1) * tq)
            p = jnp.exp2(u_ref[:, cols] - shift[:, cols])
            os_.append(_dot(vtj[h * V_ROWS:(h + 1) * V_ROWS, :], p.astype(BF16)))
        m_ref[...] = m_new
        acc_ref[...] = per_head_rows(alpha) * acc_ref[...] + jnp.concatenate(os_, axis=0)

    m_ref[...] = jnp.full(m_ref.shape, NEG_INF, F32)
    acc_ref[...] = jnp.zeros(acc_ref.shape, F32)
    produce(ub_ref, mbb_ref, n_full, True)
    produce(ua_ref, mba_ref, 0)
    consume(ub_ref, mbb_ref, n_full, True)

    def pair(t, _):
        produce(ub_ref, mbb_ref, 2 * t + 1)
        consume(ua_ref, mba_ref, 2 * t)
        produce(ua_ref, mba_ref, 2 * t + 2)
        consume(ub_ref, mbb_ref, 2 * t + 1)
        return 0

    lax.fori_loop(0, n_full // 2, pair, 0)

    @pl.when(n_full % 2 == 1)
    def _():
        consume(ua_ref, mba_ref, n_full - 1)

    acc = acc_ref[...]
    out_t = jnp.concatenate(
        [acc[h * V_ROWS:h * V_ROWS + HEAD_DIM, :] * (1.0 / acc[h * V_ROWS + HEAD_DIM:h * V_ROWS + HEAD_DIM + 1, :])
         for h in range(nh)], axis=0)
    o_ref[0] = out_t.T.astype(BF16)


def _attention(mode, q_arr, q_blk, k_arr, k_blk, vt_arr, *, batch, seq, extra=()):
    tq = tk = min(ATT_TK, seq)
    assert tq % LANES == 0 and seq % tk == 0
    dq = MLA_HEAD_PAD * N_HEADS if mode == "mla" else ATT_W
    nblk = seq // MOBA_BLOCK
    rows = N_HEADS * tq
    r3 = lambda a: a.reshape(batch, seq, a.shape[-1])
    in_specs = [pl.BlockSpec((1, tq, dq), lambda b, i: (b, i, q_blk)),
                pl.BlockSpec((1, seq, dq), lambda b, i: (b, 0, k_blk)),
                pl.BlockSpec((1, N_HEADS * V_ROWS, seq), lambda b, i: (b, 0, 0))]
    args = [r3(q_arr), r3(k_arr), vt_arr]
    scratch = []
    if mode == "fox":
        small, cumt = extra
        in_specs += [pl.BlockSpec((1, seq, LANES), lambda b, i: (b, 0, 0)),
                     pl.BlockSpec((1, SUBLANES, tq), lambda b, i: (b, 0, i))]
        args += [r3(small), cumt]
        scratch = [pltpu.VMEM((seq, N_HEADS * LANES), F32)]
    if mode == "moba":
        nbp = -(-nblk // SUBLANES) * SUBLANES
        scratch = [pltpu.VMEM((nbp, ATT_W), F32), pltpu.VMEM((nbp, rows), F32)]
    scratch += [pltpu.VMEM((dq // N_HEADS if mode == "mla" else ATT_W, rows), BF16),
                pltpu.VMEM((tk, rows), F32), pltpu.VMEM((tk, rows), F32),
                pltpu.VMEM((1, rows), F32), pltpu.VMEM((1, rows), F32),
                pltpu.VMEM((1, rows), F32),
                pltpu.VMEM((N_HEADS * V_ROWS, tq), F32)]
    out = pl.pallas_call(
        functools.partial(_attn_kernel, mode=mode, tq=tq, tk=tk, nblk=nblk, n_sel=min(MOBA_TOPK, nblk - 1)),
        grid=(batch, seq // tq),
        in_specs=in_specs,
        out_specs=pl.BlockSpec((1, tq, ATT_W), lambda b, i: (b, i, 0)),
        out_shape=jax.ShapeDtypeStruct((batch, seq, ATT_W), BF16),
        scratch_shapes=scratch,
        compiler_params=_cparams(("parallel", "arbitrary")),
        name="attn_" + mode,
    )(*args)
    return out.reshape(batch * seq, ATT_W)


def _conv_kernel(x_ref, w_ref, p_ref, o_ref, u_ref, *, chunk):
    seq = x_ref.shape[1]
    halo = 32
    u_ref[0:halo, :] = jnp.zeros((halo, CONV_CH), F32)

    def glu(c, _):
        r0 = pl.multiple_of(c * chunk, chunk)
        x = x_ref[0, pl.ds(r0, chunk), :].astype(F32)
        u_ref[pl.ds(halo + r0, chunk), :] = x[:, :CONV_CH] * _sigmoid(x[:, CONV_CH:])
        return 0

    lax.fori_loop(0, seq // chunk, glu, 0)
    bias, ln_g, ln_b = p_ref[0:1, :], p_ref[1:2, :], p_ref[2:3, :]

    def conv(c, _):
        r0 = pl.multiple_of(c * chunk, chunk)
        n = chunk + halo
        win = u_ref[pl.ds(r0, n), :]
        acc = jnp.zeros((chunk, CONV_CH), F32)
        for b in range(SUBLANES):
            shifted = win if b == 0 else pltpu.roll(win, n - b, 0)
            for a in range(halo // SUBLANES + 1):
                k = SUBLANES * a + b - (halo - (CONV_K - 1))
                if 0 <= k < CONV_K:
                    acc = acc + w_ref[k:k + 1, :] * shifted[SUBLANES * a:SUBLANES * a + chunk, :]
        acc = acc + bias
        mu = jnp.mean(acc, axis=-1, keepdims=True)
        xc = acc - mu
        var = jnp.mean(xc * xc, axis=-1, keepdims=True)
        y = xc * lax.rsqrt(var + NORM_EPS) * ln_g + ln_b
        o_ref[0, pl.ds(r0, chunk), :] = (y * _sigmoid(y)).astype(BF16)
        return 0

    lax.fori_loop(0, seq // chunk, conv, 0)


def _conv_module(conv_in, w, p, *, batch, seq):
    chunk = min(CONV_CHUNK, seq)
    out = pl.pallas_call(
        functools.partial(_conv_kernel, chunk=chunk),
        grid=(batch,),
        in_specs=[pl.BlockSpec((1, seq, 2 * CONV_CH), lambda b: (b, 0, 0)),
                  pl.BlockSpec((32, CONV_CH), lambda b: (0, 0)),
                  pl.BlockSpec((SUBLANES, CONV_CH), lambda b: (0, 0))],
        out_specs=pl.BlockSpec((1, seq, CONV_CH), lambda b: (b, 0, 0)),
        out_shape=jax.ShapeDtypeStruct((batch, seq, CONV_CH), BF16),
        scratch_shapes=[pltpu.VMEM((seq + 32, CONV_CH), F32)],
        compiler_params=_cparams(("parallel",)),
        name="conv",
    )(conv_in.reshape(batch, seq, 2 * CONV_CH), w, p)
    return out.reshape(batch * seq, CONV_CH)


def _mla_prep_kernel(cq_ref, ckv_ref, small_ref, cos_ref, sin_ref, gq_ref, gkv_ref, wq_ref, wqr_ref,
                     wk_ref, wv_ref, pk_ref, pkr_ref, q_ref, k_ref, v_ref, *, scale):
    cos, sin = cos_ref[...], sin_ref[...]
    cq = _rms(cq_ref[...].astype(F32), gq_ref[...]).astype(BF16)
    q = _dot(cq, wq_ref[...]) * cos + _dot(cq, wqr_ref[...]) * sin
    q_ref[...] = (q * scale).astype(BF16)
    ckv = _rms(ckv_ref[...].astype(F32), gkv_ref[...]).astype(BF16)
    kr = small_ref[...].astype(BF16)
    k = _dot(ckv, wk_ref[...]) + _dot(kr, pk_ref[...]) * cos + _dot(kr, pkr_ref[...]) * sin
    k_ref[...] = k.astype(BF16)
    v_ref[0] = _values_t(_dot(ckv, wv_ref[...]))


def _mla_prep(cq, ckv, small, cos_t, sin_t, gq, gkv, wq, wqr, wk, wv, pk, pkr, *, seq):
    t = cq.shape[0]
    tm = min(TOKEN_TILE, seq)
    tiles_per_seq = seq // tm
    wfull = MLA_HEAD_PAD * N_HEADS
    row = lambda i: (i, 0)
    pos = lambda i: (i % tiles_per_seq, 0)
    fixed = lambda i: (0, 0)
    return pl.pallas_call(
        functools.partial(_mla_prep_kernel, scale=float((MLA_NOPE + MLA_ROPE) ** -0.5 * LOG2E)),
        grid=(t // tm,),
        in_specs=[pl.BlockSpec((tm, MLA_Q_LORA), row), pl.BlockSpec((tm, MLA_KV_LORA), row),
                  pl.BlockSpec((tm, LANES), row),
                  pl.BlockSpec((tm, wfull), pos), pl.BlockSpec((tm, wfull), pos),
                  pl.BlockSpec((1, MLA_Q_LORA), fixed), pl.BlockSpec((1, MLA_KV_LORA), fixed),
                  pl.BlockSpec((MLA_Q_LORA, wfull), fixed), pl.BlockSpec((MLA_Q_LORA, wfull), fixed),
                  pl.BlockSpec((MLA_KV_LORA, wfull), fixed), pl.BlockSpec((MLA_KV_LORA, ATT_W), fixed),
                  pl.BlockSpec((LANES, wfull), fixed), pl.BlockSpec((LANES, wfull), fixed)],
        out_specs=[pl.BlockSpec((tm, wfull), row), pl.BlockSpec((tm, wfull), row),
                   pl.BlockSpec((1, N_HEADS * V_ROWS, tm), lambda i: (i // tiles_per_seq, 0, i % tiles_per_seq))],
        out_shape=[jax.ShapeDtypeStruct((t, wfull), BF16), jax.ShapeDtypeStruct((t, wfull), BF16),
                   jax.ShapeDtypeStruct((t // seq, N_HEADS * V_ROWS, seq), BF16)],
        compiler_params=_cparams(("parallel",)),
        name="mla_prep",
    )(cq, ckv, small, cos_t, sin_t, gq, gkv, wq, wqr, wk, wv, pk, pkr)


def _merge_kernel(h_ref, g_ref, ya_ref, yb_ref, yc_ref, yd_ref, wg_ref, bg_ref, wb_ref, wo_ref, o_ref):
    h = h_ref[...]
    xn = _rms(h, g_ref[...]).astype(BF16)
    merged = jnp.zeros(h.shape, F32)
    for i, y_ref in enumerate((ya_ref, yb_ref, yc_ref, yd_ref)):
        gate = _sigmoid(_dot(xn, wg_ref[i]) + bg_ref[i])
        merged = merged + gate * _dot(y_ref[...], wb_ref[i])
    o_ref[...] = h + _dot(merged.astype(BF16), wo_ref[...])


def _merge(h, g, ys, wg, bg, wb, wo):
    t, d = h.shape
    tm = min(TOKEN_TILE, t)
    row = lambda i: (i, 0)
    fixed2 = lambda i: (0, 0)
    fixed3 = lambda i: (0, 0, 0)
    return pl.pallas_call(
        _merge_kernel,
        grid=(t // tm,),
        in_specs=[pl.BlockSpec((tm, d), row), pl.BlockSpec((1, d), fixed2)]
        + [pl.BlockSpec((tm, BRANCH_W), row)] * N_BRANCH
        + [pl.BlockSpec((N_BRANCH, d, d), fixed3), pl.BlockSpec((N_BRANCH, 1, d), fixed3),
           pl.BlockSpec((N_BRANCH, BRANCH_W, d), fixed3), pl.BlockSpec((d, d), fixed2)],
        out_specs=pl.BlockSpec((tm, d), row),
        out_shape=jax.ShapeDtypeStruct((t, d), F32),
        compiler_params=_cparams(("parallel",)),
        name="merge",
    )(h, g, *ys, wg, bg, wb, wo)


def _route_kernel(h_ref, g_ref, wr_ref, br_ref, hn_ref, ri_ref, rw_ref, cnt_ref, carry_ref):
    i = pl.program_id(0)
    tm = h_ref.shape[0]
    hn = _rms(h_ref[...], g_ref[...])
    hn_ref[...] = hn
    lg = jnp.dot(hn, wr_ref[...], precision=lax.Precision.HIGHEST, preferred_element_type=F32) + br_ref[...]
    lane = lax.broadcasted_iota(I32, lg.shape, 1)
    big = jnp.int32(4 * LANES)

    def top1(vals):
        mx = jnp.max(vals, axis=1, keepdims=True)
        idx = jnp.min(jnp.where(vals == mx, lane, big), axis=1, keepdims=True)
        return mx, idx

    is_g = (lane >= N_EXPERTS) & (lane < N_EXPERTS + N_GROUPS)
    gmax, gidx = top1(jnp.where(is_g, lg, NEG_INF))
    p_sel = 1.0 / jnp.sum(jnp.where(is_g, jnp.exp(lg - gmax), 0.0), axis=1, keepdims=True)
    lo = (gidx - N_EXPERTS) * EXPERTS_PER_GROUP
    el = jnp.where((lane >= lo) & (lane < lo + EXPERTS_PER_GROUP), lg, NEG_INF)
    l1, i1 = top1(el)
    l2, i2 = top1(jnp.where(lane == i1, NEG_INF, el))
    e = jnp.exp(l2 - l1)
    w1 = p_sel / (1.0 + e)
    w2 = p_sel * e / (1.0 + e)

    @pl.when(i == 0)
    def _():
        carry_ref[...] = jnp.zeros_like(carry_ref)

    hit1, hit2 = lane == i1, lane == i2
    onehot = jnp.where(hit1 | hit2, 1.0, 0.0)
    r_i = lax.broadcasted_iota(I32, (tm, tm), 0)
    c_i = lax.broadcasted_iota(I32, (tm, tm), 1)
    before = _dot(jnp.where(c_i < r_i, 1.0, 0.0).astype(BF16), onehot.astype(BF16)) + carry_ref[...]
    rank1 = jnp.sum(jnp.where(hit1, before, 0.0), axis=1, keepdims=True).astype(I32)
    rank2 = jnp.sum(jnp.where(hit2, before, 0.0), axis=1, keepdims=True).astype(I32)
    carry_ref[...] = carry_ref[...] + jnp.sum(onehot, axis=0, keepdims=True)
    cnt_ref[...] = carry_ref[...]
    ri_ref[...] = jnp.where(lane == 0, i1, jnp.where(lane == 1, i2, jnp.where(lane == 2, rank1,
                            jnp.where(lane == 3, rank2, 0))))
    rw_ref[...] = jnp.where(lane == 0, w1, jnp.where(lane == 1, w2, 0.0))


def _route(h, g, wr, br):
    t, d = h.shape
    tm = min(TOKEN_TILE, t)
    row = lambda i: (i, 0)
    fixed = lambda i: (0, 0)
    return pl.pallas_call(
        _route_kernel,
        grid=(t // tm,),
        in_specs=[pl.BlockSpec((tm, d), row), pl.BlockSpec((1, d), fixed),
                  pl.BlockSpec((d, LANES), fixed), pl.BlockSpec((1, LANES), fixed)],
        out_specs=[pl.BlockSpec((tm, d), row), pl.BlockSpec((tm, LANES), row),
                   pl.BlockSpec((tm, LANES), row), pl.BlockSpec((1, LANES), fixed)],
        out_shape=[jax.ShapeDtypeStruct((t, d), F32), jax.ShapeDtypeStruct((t, LANES), I32),
                   jax.ShapeDtypeStruct((t, LANES), F32), jax.ShapeDtypeStruct((1, LANES), F32)],
        scratch_shapes=[pltpu.VMEM((1, LANES), F32)],
        compiler_params=_cparams(("arbitrary",)),
        name="route",
    )(h, g, wr, br)


def _ffn_kernel(blk_e_ref, blk_on_ref, src_cur_ref, src_nxt_ref, dst_ref, hn_hbm, w13_ref, w2_ref, y_hbm,
                src_s, dst_s, xbuf, obuf, sem_idx, sem_in, sem_out):
    i = pl.program_id(0)
    n = pl.num_programs(0)
    slot = i % 2
    rows = xbuf.shape[1]

    def to_smem(vmem_ref, smem_ref, sem):
        cp = pltpu.make_async_copy(vmem_ref.at[0, 0], smem_ref, sem)
        cp.start()
        cp.wait()

    def start_gather(s):
        def row(r, _):
            pltpu.make_async_copy(hn_hbm.at[pl.ds(src_s[r], 1)], xbuf.at[s, pl.ds(r, 1)], sem_in.at[s]).start()
            return 0
        lax.fori_loop(0, rows, row, 0, unroll=True)

    def start_scatter(s):
        def row(r, _):
            pltpu.make_async_copy(obuf.at[s, pl.ds(r, 1)], y_hbm.at[pl.ds(dst_s[r], 1)], sem_out.at[s]).start()
            return 0
        lax.fori_loop(0, rows, row, 0, unroll=True)

    def wait_gather(s):
        pltpu.make_async_copy(hn_hbm.at[pl.ds(0, rows)], xbuf.at[s], sem_in.at[s]).wait()

    def wait_scatter(s):
        pltpu.make_async_copy(obuf.at[s], y_hbm.at[pl.ds(0, rows)], sem_out.at[s]).wait()

    @pl.when(i == 0)
    def _():
        to_smem(src_cur_ref, src_s, sem_idx.at[0])
        start_gather(0)

    to_smem(src_nxt_ref, src_s, sem_idx.at[0])
    to_smem(dst_ref, dst_s, sem_idx.at[1])

    @pl.when(i + 1 < n)
    def _():
        start_gather(1 - slot)

    wait_gather(slot)

    @pl.when(i >= 2)
    def _():
        wait_scatter(slot)

    @pl.when(blk_on_ref[i] == 1)
    def _():
        x = xbuf[slot].astype(BF16)
        ab = _dot(x, w13_ref[0])
        a, b = ab[:, :EXPERT_FF], ab[:, EXPERT_FF:]
        act = (a * _sigmoid(a) * b).astype(BF16)
        obuf[slot] = _dot(act, w2_ref[0])

    @pl.when(blk_on_ref[i] == 0)
    def _():
        obuf[slot] = jnp.zeros(obuf.shape[1:], F32)

    start_scatter(slot)

    @pl.when(i == n - 1)
    def _():
        wait_scatter(slot)

        @pl.when(n >= 2)
        def _():
            wait_scatter(1 - slot)


def _expert_ffn(hn, src_tok, dst_row, blk_e, blk_on, w13, w2, *, y_rows):
    t, d = hn.shape
    n_blk = src_tok.shape[0]
    idx_blk = (1, 1, FFN_ROWS)
    grid_spec = pltpu.PrefetchScalarGridSpec(
        num_scalar_prefetch=2,
        grid=(n_blk,),
        in_specs=[pl.BlockSpec(idx_blk, lambda i, be, bo: (i, 0, 0)),
                  pl.BlockSpec(idx_blk, lambda i, be, bo: (jnp.minimum(i + 1, n_blk - 1), 0, 0)),
                  pl.BlockSpec(idx_blk, lambda i, be, bo: (i, 0, 0)),
                  pl.BlockSpec(memory_space=pl.ANY),
                  pl.BlockSpec((1, d, 2 * EXPERT_FF), lambda i, be, bo: (be[i], 0, 0)),
                  pl.BlockSpec((1, EXPERT_FF, d), lambda i, be, bo: (be[i], 0, 0))],
        out_specs=pl.BlockSpec(memory_space=pl.ANY),
        scratch_shapes=[pltpu.SMEM((FFN_ROWS,), I32), pltpu.SMEM((FFN_ROWS,), I32),
                        pltpu.VMEM((2, FFN_ROWS, d), F32), pltpu.VMEM((2, FFN_ROWS, d), F32),
                        pltpu.SemaphoreType.DMA((2,)), pltpu.SemaphoreType.DMA((2,)),
                        pltpu.SemaphoreType.DMA((2,))],
    )
    return pl.pallas_call(
        _ffn_kernel,
        grid_spec=grid_spec,
        out_shape=jax.ShapeDtypeStruct((y_rows, d), F32),
        compiler_params=_cparams(("arbitrary",)),
        name="expert_ffn",
    )(blk_e, blk_on, src_tok, src_tok, dst_row, hn, w13, w2)


def _final_kernel(h_ref, ya_ref, yb_ref, rw_ref, g_ref, o_ref):
    o_ref[...] = _rms(_moe_combine(h_ref, (ya_ref, yb_ref, rw_ref)), g_ref[...])


def _final_norm(h, y, rw, g):
    t, d = h.shape
    tm = min(TOKEN_TILE, t)
    return pl.pallas_call(
        _final_kernel,
        grid=(t // tm,),
        in_specs=[pl.BlockSpec((tm, d), lambda i: (i, 0))] + _moe_specs(t, tm, d)
        + [pl.BlockSpec((1, d), lambda i: (0, 0))],
        out_specs=pl.BlockSpec((tm, d), lambda i: (i, 0)),
        out_shape=jax.ShapeDtypeStruct((t, d), F32),
        compiler_params=_cparams(("parallel",)),
        name="final_norm",
    )(h, y, y, rw, g)


def _prep_inproj(w_in, b_forget):
    d = w_in.shape[0]
    o = np.cumsum([0, 3 * ATT_W, 3 * ATT_W, N_HEADS, 2 * CONV_CH, MLA_Q_LORA, MLA_KV_LORA, MLA_ROPE]).tolist()
    parts = [w_in[:, o[0]:o[1]], w_in[:, o[1]:o[2]], w_in[:, o[3]:o[4]],
             w_in[:, o[4]:o[5]], w_in[:, o[5]:o[6]], w_in[:, o[6]:o[7]], w_in[:, o[2]:o[3]],
             jnp.zeros((d, C_END - C_SMALL - MLA_ROPE - N_HEADS), F32)]
    w = jnp.concatenate(parts, axis=1).astype(BF16)
    bf = jnp.zeros((1, LANES), F32).at[0, SM_CUM:SM_CUM + N_HEADS].set(b_forget)
    return w, bf


def _rope_placement():
    hp, half = MLA_HEAD_PAD, MLA_ROPE // 2
    pk = np.zeros((LANES, hp * N_HEADS), np.float32)
    pkr = np.zeros((LANES, hp * N_HEADS), np.float32)
    for h in range(N_HEADS):
        for i in range(MLA_ROPE):
            pk[SM_ROPE + i, h * hp + MLA_NOPE + i] = 1.0
        for i in range(half):
            pkr[SM_ROPE + half + i, h * hp + MLA_NOPE + i] = -1.0
            pkr[SM_ROPE + i, h * hp + MLA_NOPE + half + i] = 1.0
    return jnp.asarray(pk, BF16), jnp.asarray(pkr, BF16)


def _prep_mla(wuq, wukv):
    hp, half = MLA_HEAD_PAD, MLA_ROPE // 2
    qd = MLA_NOPE + MLA_ROPE
    zq = jnp.zeros((MLA_Q_LORA, hp - qd), F32)
    wq, wqr, wk, wv = [], [], [], []
    for h in range(N_HEADS):
        nope = wuq[:, h * qd:h * qd + MLA_NOPE]
        pe = wuq[:, h * qd + MLA_NOPE:(h + 1) * qd]
        wq += [nope, pe, zq]
        wqr += [jnp.zeros_like(nope), -pe[:, half:], pe[:, :half], zq]
        kv = wukv[:, h * (MLA_NOPE + MLA_V):(h + 1) * (MLA_NOPE + MLA_V)]
        wk += [kv[:, :MLA_NOPE], jnp.zeros((MLA_KV_LORA, hp - MLA_NOPE), F32)]
        wv += [kv[:, MLA_NOPE:]]
    cat = lambda xs: jnp.concatenate(xs, axis=1).astype(BF16)
    return cat(wq), cat(wqr), cat(wk), cat(wv)


def _rope_tables(seq):
    inv_freq = ROPE_THETA ** (-jnp.arange(0, MLA_ROPE, 2, dtype=F32) / MLA_ROPE)
    ang = jnp.arange(seq, dtype=F32)[:, None] * inv_freq[None, :]
    cos, sin = jnp.cos(ang), jnp.sin(ang)
    one = jnp.ones((seq, MLA_NOPE), F32)
    zero = jnp.zeros((seq, MLA_NOPE), F32)
    pad = jnp.zeros((seq, MLA_HEAD_PAD - MLA_NOPE - MLA_ROPE), F32)
    cos_h = jnp.concatenate([one, cos, cos, pad], axis=1)
    sin_h = jnp.concatenate([zero, sin, sin, pad], axis=1)
    return jnp.tile(cos_h, (1, N_HEADS)), jnp.tile(sin_h, (1, N_HEADS))


def kernel(x, norm1_g, w_in, b_forget, conv_w, conv_b, conv_ln_g, conv_ln_b, mla_gq, mla_wuq, mla_gkv,
           mla_wukv, w_gate, b_gate, w_branch, w_out, norm2_g, w_router_group, b_router_group,
           w_router_expert, b_router_expert, w_exp_gate, w_exp_up, w_exp_down, final_g):
    batch, seq, d = x.shape
    t = batch * seq
    depth = w_in.shape[0]
    assert seq % MOBA_BLOCK == 0 and t % TOKEN_TILE == 0
    cos_t, sin_t = _rope_tables(seq)
    pk, pkr = _rope_placement()
    n_slots = 2 * t
    p_rows = n_slots + N_EXPERTS * FFN_ROWS
    n_blk = p_rows // FFN_ROWS

    h = x.reshape(t, d)
    moe = None
    for l in range(depth):
        w_l, bf_l = _prep_inproj(w_in[l], b_forget[l])
        outs = _inproj(h, norm1_g[l][None], w_l, bf_l, seq=seq, moe=moe)
        if moe is not None:
            h, *outs = outs
        moba, moba_vt, fox, fox_vt, conv_in, cq, ckv, small, cumt = outs
        y_a = _attention("moba", moba, 0, moba, 1, moba_vt, batch=batch, seq=seq)
        y_b = _attention("fox", fox, 0, fox, 1, fox_vt, batch=batch, seq=seq, extra=(small, cumt))
        conv_wp = jnp.zeros((32, CONV_CH), F32).at[:CONV_K].set(conv_w[l])
        conv_p = jnp.zeros((SUBLANES, CONV_CH), F32).at[0].set(conv_b[l]).at[1].set(conv_ln_g[l]).at[2].set(
            conv_ln_b[l])
        y_c = _conv_module(conv_in, conv_wp, conv_p, batch=batch, seq=seq)
        wq, wqr, wk, wv = _prep_mla(mla_wuq[l], mla_wukv[l])
        qd, kd, vd = _mla_prep(cq, ckv, small, cos_t, sin_t, mla_gq[l][None], mla_gkv[l][None],
                               wq, wqr, wk, wv, pk, pkr, seq=seq)
        y_d = _attention("mla", qd, 0, kd, 0, vd, batch=batch, seq=seq)
        h = _merge(h, norm1_g[l][None], (y_a, y_b, y_c, y_d), w_gate[l].astype(BF16), b_gate[l][:, None, :],
                   w_branch[l].astype(BF16), w_out[l].astype(BF16))

        wr = jnp.zeros((d, LANES), F32).at[:, :N_EXPERTS].set(w_router_expert[l]).at[
            :, N_EXPERTS:N_EXPERTS + N_GROUPS].set(w_router_group[l])
        br = jnp.zeros((1, LANES), F32).at[0, :N_EXPERTS].set(b_router_expert[l]).at[
            0, N_EXPERTS:N_EXPERTS + N_GROUPS].set(b_router_group[l])
        hn, ri, rw, cnt = _route(h, norm2_g[l][None], wr, br)
        counts = cnt[0, :N_EXPERTS].astype(I32)
        pcounts = ((counts + FFN_ROWS - 1) // FFN_ROWS) * FFN_ROWS
        pends = jnp.cumsum(pcounts)
        pstarts = pends - pcounts
        dest = (pstarts[ri[:, 0:2]] + ri[:, 2:4]).T.reshape(-1)
        tok = jnp.tile(jnp.arange(t, dtype=I32), 2)
        src_tok = jnp.zeros((p_rows,), I32).at[dest].set(tok, unique_indices=True)
        slot_at = jnp.full((p_rows,), -1, I32).at[dest].set(jnp.arange(n_slots, dtype=I32), unique_indices=True)
        pad_rank = jnp.cumsum((slot_at < 0).astype(I32)) - 1
        dst_row = jnp.where(slot_at >= 0, slot_at, n_slots + pad_rank)
        blk_start = jnp.arange(n_blk, dtype=I32) * FFN_ROWS
        blk_e = jnp.minimum(jnp.sum(blk_start[:, None] >= pends[None, :], axis=1), N_EXPERTS - 1).astype(I32)
        blk_on = (blk_start < pends[-1]).astype(I32)
        w13 = jnp.concatenate([w_exp_gate[l], w_exp_up[l]], axis=-1).astype(BF16)
        y = _expert_ffn(hn, src_tok.reshape(n_blk, 1, FFN_ROWS), dst_row.reshape(n_blk, 1, FFN_ROWS), blk_e, blk_on,
                        w13, w_exp_down[l].astype(BF16), y_rows=p_rows)
        moe = (y, rw)
    return _final_norm(h, *moe, final_g[None]).reshape(batch, seq, d)
```

```python
import functools

import numpy as np
import jax
import jax.numpy as jnp
from jax import lax
from jax.experimental import pallas as pl
from jax.experimental.pallas import tpu as pltpu

F32 = jnp.float32
BF16 = jnp.bfloat16
I32 = jnp.int32

HEAD_DIM = 64
N_HEADS = 4
ATT_W = N_HEADS * HEAD_DIM
MOBA_BLOCK = 256
MOBA_TOPK = 3
CONV_CH = 256
CONV_K = 31
MLA_Q_LORA = 256
MLA_KV_LORA = 128
MLA_NOPE = 64
MLA_ROPE = 32
MLA_V = 64
MLA_HEAD_PAD = 128
ROPE_THETA = 10000.0
BRANCH_W = 256
N_BRANCH = 4
N_GROUPS = 4
EXPERTS_PER_GROUP = 8
N_EXPERTS = N_GROUPS * EXPERTS_PER_GROUP
EXPERT_FF = 256
NORM_EPS = 1e-6
NEG_INF = -1e30
LOG2E = 1.4426950408889634

LANES = 128
SUBLANES = 8
VMEM_LIMIT = 56 * 1024 * 1024

TOKEN_TILE = 512
ATT_TK = 256
V_ROWS = 80
CONV_CHUNK = 128
FFN_ROWS = 256

C_MOBA, C_FOX, C_CONV, C_CQ, C_CKV, C_SMALL, C_END = 0, 768, 1536, 2048, 2304, 2432, 2560
SM_ROPE = 0
SM_CUM = 32


def _cparams(sem):
    return pltpu.CompilerParams(dimension_semantics=sem, vmem_limit_bytes=VMEM_LIMIT)


def _rms(x, g):
    return x * lax.rsqrt(jnp.mean(x * x, axis=-1, keepdims=True) + NORM_EPS) * g


def _sigmoid(x):
    return 1.0 / (1.0 + jnp.exp(-x))


def _dot(a, b):
    return jnp.dot(a, b, preferred_element_type=F32)


def _values_t(v):
    vt = v.T
    ones = jnp.ones((V_ROWS - HEAD_DIM, vt.shape[1]), F32)
    parts = [x for h in range(N_HEADS) for x in (vt[h * HEAD_DIM:(h + 1) * HEAD_DIM, :], ones)]
    return jnp.concatenate(parts, axis=0).astype(BF16)


def _moe_combine(h_ref, moe_refs):
    ya_ref, yb_ref, rw_ref = moe_refs
    rw = rw_ref[...]
    return h_ref[...] + rw[:, 0:1] * _load_row_tiles(ya_ref) + rw[:, 1:2] * _load_row_tiles(yb_ref)


def _inproj_kernel(*refs, tiles_per_seq, combine):
    if combine:
        h_ref, *moe_refs, g_ref, w_ref, bf_ref, hout_ref = refs[:8]
        refs = refs[8:]
        h = _moe_combine(h_ref, moe_refs)
        hout_ref[...] = h
    else:
        h_ref, g_ref, w_ref, bf_ref = refs[:4]
        refs = refs[4:]
        h = h_ref[...]
    moba_ref, mvt_ref, fox_ref, fvt_ref, conv_ref, cq_ref, ckv_ref, small_ref, cumt_ref, carry_ref = refs
    i = pl.program_id(0)
    tm = h_ref.shape[0]
    xn = _rms(h, g_ref[...]).astype(BF16)

    def proj(a, b):
        return _dot(xn, w_ref[:, a:b])

    for qk_ref, vt_ref, c0 in ((moba_ref, mvt_ref, C_MOBA), (fox_ref, fvt_ref, C_FOX)):
        qk_ref[:, :ATT_W] = (proj(c0, c0 + ATT_W) * (HEAD_DIM ** -0.5 * LOG2E)).astype(BF16)
        qk_ref[:, ATT_W:] = proj(c0 + ATT_W, c0 + 2 * ATT_W).astype(BF16)
        vt_ref[0] = _values_t(proj(c0 + 2 * ATT_W, c0 + 3 * ATT_W))
    conv_ref[...] = proj(C_CONV, C_CQ).astype(BF16)
    cq_ref[...] = proj(C_CQ, C_CKV).astype(BF16)
    ckv_ref[...] = proj(C_CKV, C_SMALL).astype(BF16)
    sm = proj(C_SMALL, C_END)

    f = sm + bf_ref[...]
    c = jnp.minimum(f, 0.0) - jnp.log(1.0 + jnp.exp(-jnp.abs(f)))
    row = lax.broadcasted_iota(I32, c.shape, 0)
    sh = 1
    while sh < tm:
        c = c + jnp.where(row >= sh, pltpu.roll(c, sh, 0), 0.0)
        sh *= 2

    @pl.when(i % tiles_per_seq == 0)
    def _():
        carry_ref[...] = jnp.zeros_like(carry_ref)

    c = c + carry_ref[...]
    carry_ref[...] = c[tm - 1:tm, :]
    lane = lax.broadcasted_iota(I32, c.shape, 1)
    small_ref[...] = jnp.where((lane >= SM_CUM) & (lane < SM_CUM + N_HEADS), c, sm)
    cumt_ref[0] = c.T[SM_CUM:SM_CUM + SUBLANES, :]


def _moe_specs(t, tm, d):
    assert d == ROW_TILE * LANES
    return [pl.BlockSpec((tm * ROW_TILE, LANES), lambda i: (i, 0)),
            pl.BlockSpec((tm * ROW_TILE, LANES), lambda i: (i + t // tm, 0)),
            pl.BlockSpec((tm, LANES), lambda i: (i, 0))]


def _inproj(h, g, w, bf, *, seq, moe=None):
    t, d = h.shape
    tm = min(TOKEN_TILE, seq)
    tiles_per_seq = seq // tm
    nb = t // seq
    row = lambda i: (i, 0)
    fixed = lambda i: (0, 0)
    seq_t = lambda i: (i // tiles_per_seq, 0, i % tiles_per_seq)
    tok = lambda wdt, dt: (pl.BlockSpec((tm, wdt), row), jax.ShapeDtypeStruct((t, wdt), dt))
    chan = lambda c, dt: (pl.BlockSpec((1, c, tm), seq_t), jax.ShapeDtypeStruct((nb, c, seq), dt))
    outs = [tok(2 * ATT_W, BF16), chan(N_HEADS * V_ROWS, BF16), tok(2 * ATT_W, BF16), chan(N_HEADS * V_ROWS, BF16),
            tok(C_CQ - C_CONV, BF16), tok(C_CKV - C_CQ, BF16), tok(C_SMALL - C_CKV, BF16),
            tok(C_END - C_SMALL, F32), chan(SUBLANES, F32)]
    if moe is not None:
        outs = [tok(d, F32)] + outs
    return pl.pallas_call(
        functools.partial(_inproj_kernel, tiles_per_seq=tiles_per_seq, combine=moe is not None),
        grid=(t // tm,),
        in_specs=[pl.BlockSpec((tm, d), row)] + (_moe_specs(t, tm, d) if moe is not None else [])
        + [pl.BlockSpec((1, d), fixed), pl.BlockSpec((d, C_END), fixed), pl.BlockSpec((1, LANES), fixed)],
        out_specs=[o[0] for o in outs],
        out_shape=[o[1] for o in outs],
        scratch_shapes=[pltpu.VMEM((1, LANES), F32)],
        compiler_params=_cparams(("arbitrary",)),
        name="inproj",
    )(h, *((moe[0], moe[0], moe[1]) if moe is not None else ()), g, w, bf)


def _attn_kernel(*refs, mode, tq, tk, nblk, n_sel):
    qs_ref, ua_ref, ub_ref, mba_ref, mbb_ref, m_ref, acc_ref = refs[-7:]
    refs = refs[:-7]
    if mode == "fox":
        q_ref, k_ref, vt_ref, small_ref, cumt_ref, o_ref, csb_ref = refs
    elif mode == "moba":
        q_ref, k_ref, vt_ref, o_ref, kmean_ref, selb_ref = refs
    else:
        q_ref, k_ref, vt_ref, o_ref = refs
    i = pl.program_id(1)
    nh = N_HEADS
    rows = nh * tq
    seq = k_ref.shape[1]
    n_full = (i * tq) // tk
    q0 = i * tq

    q = q_ref[0]
    if mode == "mla":
        for h in range(nh):
            qs_ref[:, h * tq:(h + 1) * tq] = q[:, h * MLA_HEAD_PAD:(h + 1) * MLA_HEAD_PAD].astype(F32).T.astype(BF16)
    else:
        qt = q.astype(F32).T.astype(BF16)
        sub_h = lax.shift_right_logical(lax.broadcasted_iota(I32, (ATT_W, tq), 0), 6)
        for h in range(nh):
            qs_ref[:, h * tq:(h + 1) * tq] = jnp.where(sub_h == h, qt, jnp.zeros_like(qt))

    r_in = lax.broadcasted_iota(I32, (tk, rows), 1) & (tq - 1)
    c_in = lax.broadcasted_iota(I32, (tk, rows), 0)

    def per_head_rows(x):
        return jnp.concatenate(
            [jnp.broadcast_to(x[:, h * tq:(h + 1) * tq], (V_ROWS, tq)) for h in range(nh)], axis=0)

    if mode == "fox":
        @pl.when(i == 0)
        def _():
            def fill(c, _):
                r0 = pl.multiple_of(c * tk, tk)
                cs = small_ref[0, pl.ds(r0, tk), :]
                for h in range(nh):
                    csb_ref[pl.ds(r0, tk), h * LANES:(h + 1) * LANES] = jnp.broadcast_to(
                        -LOG2E * cs[:, SM_CUM + h:SM_CUM + h + 1], (tk, LANES))
                return 0
            lax.fori_loop(0, seq // tk, fill, 0)

        ct = cumt_ref[0] * LOG2E
        c_t = jnp.concatenate([ct[h:h + 1, :] for h in range(nh)], axis=1)
    if mode == "moba":
        slopes = [LOG2E * 2.0 ** (-8.0 * (h + 1) / nh) for h in range(nh)]
        slope_row = jnp.concatenate([jnp.full((1, tq), s, F32) for s in slopes], axis=1)
        alibi_key = slope_row * c_in.astype(F32)
        r_row = (lax.broadcasted_iota(I32, (1, rows), 1) & (tq - 1)).astype(F32)
        nbp = kmean_ref.shape[0]

        @pl.when(i == 0)
        def _():
            kmean_ref[...] = jnp.zeros_like(kmean_ref)
            for n in range(nblk):
                kb = k_ref[0, n * MOBA_BLOCK:(n + 1) * MOBA_BLOCK, :].astype(F32)
                kmean_ref[n:n + 1, :] = jnp.mean(kb, axis=0, keepdims=True)

        km = kmean_ref[...]
        km_hi = km.astype(BF16)
        km_lo = (km - km_hi.astype(F32)).astype(BF16)
        cur = (i * tq) // MOBA_BLOCK
        qs = qs_ref[...]
        gate = _dot(km_hi, qs) + _dot(km_lo, qs)
        n_iota = lax.broadcasted_iota(I32, gate.shape, 0)
        gate = jnp.where(n_iota < cur, gate, NEG_INF)
        rank = jnp.zeros(gate.shape, I32)
        for n2 in range(nblk):
            gb = gate[n2:n2 + 1, :]
            ahead = jnp.where(gb > gate, 1, jnp.where((gb == gate) & (n_iota > n2), 1, 0))
            rank = rank + ahead
        selb_ref[...] = jnp.where((rank < n_sel) & (n_iota < cur), 0.0, NEG_INF)

    def produce(u_ref, mb_ref, j, diagonal=False):
        k0 = pl.multiple_of(j * tk, tk)
        kj = k_ref[0, pl.ds(k0, tk), :]
        if mode == "mla":
            u = jnp.concatenate(
                [_dot(kj[:, h * MLA_HEAD_PAD:(h + 1) * MLA_HEAD_PAD], qs_ref[:, h * tq:(h + 1) * tq])
                 for h in range(nh)], axis=1)
        else:
            u = _dot(kj, qs_ref[...])
        if mode == "fox":
            cs = csb_ref[pl.ds(k0, tk), :]
            u = u + jnp.concatenate([cs[:, h * LANES:(h + 1) * LANES] for h in range(nh)
                                     for _ in range(tq // LANES)], axis=1)
        if mode == "moba":
            u = u + alibi_key
        if diagonal:
            u = jnp.where(c_in + j * tk <= r_in + q0, u, NEG_INF)
        u_ref[...] = u
        mb_ref[...] = jnp.max(u, axis=0, keepdims=True)

    def consume(u_ref, mb_ref, j, diagonal=False):
        vtj = vt_ref[0, :, pl.ds(pl.multiple_of(j * tk, tk), tk)]
        m = m_ref[...]
        m_blk = mb_ref[...]
        if mode == "fox":
            row = c_t
        if mode == "moba":
            row = slope_row * ((j * tk - q0).astype(F32) - r_row)
            if not diagonal:
                row = row + selb_ref[pl.ds(j, 1), :]
        if mode != "mla":
            m_blk = m_blk + row
        m_new = jnp.maximum(m, m_blk)
        alpha = jnp.exp2(m - m_new)
        shift = m_new - row if mode != "mla" else m_new
        os_ = []
        for h in range(nh):
            cols = slice(h * tq, (h + 1) * tq)
            p = jnp.exp2(u_ref[:, cols] - shift[:, cols])
            os_.append(_dot(vtj[h * V_ROWS:(h + 1) * V_ROWS, :], p.astype(BF16)))
        m_ref[...] = m_new
        acc_ref[...] = per_head_rows(alpha) * acc_ref[...] + jnp.concatenate(os_, axis=0)

    m_ref[...] = jnp.full(m_ref.shape, NEG_INF, F32)
    acc_ref[...] = jnp.zeros(acc_ref.shape, F32)
    produce(ub_ref, mbb_ref, n_full, True)
    produce(ua_ref, mba_ref, 0)
    consume(ub_ref, mbb_ref, n_full, True)

    def pair(t, _):
        produce(ub_ref, mbb_ref, 2 * t + 1)
        consume(ua_ref, mba_ref, 2 * t)
        produce(ua_ref, mba_ref, 2 * t + 2)
        consume(ub_ref, mbb_ref, 2 * t + 1)
        return 0

    lax.fori_loop(0, n_full // 2, pair, 0)

    @pl.when(n_full % 2 == 1)
    def _():
        consume(ua_ref, mba_ref, n_full - 1)

    acc = acc_ref[...]
    out_t = jnp.concatenate(
        [acc[h * V_ROWS:h * V_ROWS + HEAD_DIM, :] * (1.0 / acc[h * V_ROWS + HEAD_DIM:h * V_ROWS + HEAD_DIM + 1, :])
         for h in range(nh)], axis=0)
    o_ref[0] = out_t.T.astype(BF16)


def _attention(mode, q_arr, q_blk, k_arr, k_blk, vt_arr, *, batch, seq, extra=()):
    tq = tk = min(ATT_TK, seq)
    assert tq % LANES == 0 and seq % tk == 0
    dq = MLA_HEAD_PAD * N_HEADS if mode == "mla" else ATT_W
    nblk = seq // MOBA_BLOCK
    rows = N_HEADS * tq
    r3 = lambda a: a.reshape(batch, seq, a.shape[-1])
    in_specs = [pl.BlockSpec((1, tq, dq), lambda b, i: (b, i, q_blk)),
                pl.BlockSpec((1, seq, dq), lambda b, i: (b, 0, k_blk)),
                pl.BlockSpec((1, N_HEADS * V_ROWS, seq), lambda b, i: (b, 0, 0))]
    args = [r3(q_arr), r3(k_arr), vt_arr]
    scratch = []
    if mode == "fox":
        small, cumt = extra
        in_specs += [pl.BlockSpec((1, seq, LANES), lambda b, i: (b, 0, 0)),
                     pl.BlockSpec((1, SUBLANES, tq), lambda b, i: (b, 0, i))]
        args += [r3(small), cumt]
        scratch = [pltpu.VMEM((seq, N_HEADS * LANES), F32)]
    if mode == "moba":
        nbp = -(-nblk // SUBLANES) * SUBLANES
        scratch = [pltpu.VMEM((nbp, ATT_W), F32), pltpu.VMEM((nbp, rows), F32)]
    scratch += [pltpu.VMEM((dq // N_HEADS if mode == "mla" else ATT_W, rows), BF16),
                pltpu.VMEM((tk, rows), F32), pltpu.VMEM((tk, rows), F32),
                pltpu.VMEM((1, rows), F32), pltpu.VMEM((1, rows), F32),
                pltpu.VMEM((1, rows), F32),
                pltpu.VMEM((N_HEADS * V_ROWS, tq), F32)]
    out = pl.pallas_call(
        functools.partial(_attn_kernel, mode=mode, tq=tq, tk=tk, nblk=nblk, n_sel=min(MOBA_TOPK, nblk - 1)),
        grid=(batch, seq // tq),
        in_specs=in_specs,
        out_specs=pl.BlockSpec((1, tq, ATT_W), lambda b, i: (b, i, 0)),
        out_shape=jax.ShapeDtypeStruct((batch, seq, ATT_W), BF16),
        scratch_shapes=scratch,
        compiler_params=_cparams(("parallel", "arbitrary")),
        name="attn_" + mode,
    )(*args)
    return out.reshape(batch * seq, ATT_W)


def _conv_kernel(x_ref, w_ref, p_ref, o_ref, u_ref, *, chunk):
    seq = x_ref.shape[1]
    halo = 32
    u_ref[0:halo, :] = jnp.zeros((halo, CONV_CH), F32)

    def glu(c, _):
        r0 = pl.multiple_of(c * chunk, chunk)
        x = x_ref[0, pl.ds(r0, chunk), :].astype(F32)
        u_ref[pl.ds(halo + r0, chunk), :] = x[:, :CONV_CH] * _sigmoid(x[:, CONV_CH:])
        return 0

    lax.fori_loop(0, seq // chunk, glu, 0)
    bias, ln_g, ln_b = p_ref[0:1, :], p_ref[1:2, :], p_ref[2:3, :]

    def conv(c, _):
        r0 = pl.multiple_of(c * chunk, chunk)
        n = chunk + halo
        win = u_ref[pl.ds(r0, n), :]
        acc = jnp.zeros((chunk, CONV_CH), F32)
        for b in range(SUBLANES):
            shifted = win if b == 0 else pltpu.roll(win, n - b, 0)
            for a in range(halo // SUBLANES + 1):
                k = SUBLANES * a + b - (halo - (CONV_K - 1))
                if 0 <= k < CONV_K:
                    acc = acc + w_ref[k:k + 1, :] * shifted[SUBLANES * a:SUBLANES * a + chunk, :]
        acc = acc + bias
        mu = jnp.mean(acc, axis=-1, keepdims=True)
        xc = acc - mu
        var = jnp.mean(xc * xc, axis=-1, keepdims=True)
        y = xc * lax.rsqrt(var + NORM_EPS) * ln_g + ln_b
        o_ref[0, pl.ds(r0, chunk), :] = (y * _sigmoid(y)).astype(BF16)
        return 0

    lax.fori_loop(0, seq // chunk, conv, 0)


def _conv_module(conv_in, w, p, *, batch, seq):
    chunk = min(CONV_CHUNK, seq)
    out = pl.pallas_call(
        functools.partial(_conv_kernel, chunk=chunk),
        grid=(batch,),
        in_specs=[pl.BlockSpec((1, seq, 2 * CONV_CH), lambda b: (b, 0, 0)),
                  pl.BlockSpec((32, CONV_CH), lambda b: (0, 0)),
                  pl.BlockSpec((SUBLANES, CONV_CH), lambda b: (0, 0))],
        out_specs=pl.BlockSpec((1, seq, CONV_CH), lambda b: (b, 0, 0)),
        out_shape=jax.ShapeDtypeStruct((batch, seq, CONV_CH), BF16),
        scratch_shapes=[pltpu.VMEM((seq + 32, CONV_CH), F32)],
        compiler_params=_cparams(("parallel",)),
        name="conv",
    )(conv_in.reshape(batch, seq, 2 * CONV_CH), w, p)
    return out.reshape(batch * seq, CONV_CH)


def _mla_prep_kernel(cq_ref, ckv_ref, small_ref, cos_ref, sin_ref, gq_ref, gkv_ref, wq_ref, wqr_ref,
                     wk_ref, wv_ref, pk_ref, pkr_ref, q_ref, k_ref, v_ref, *, scale):
    cos, sin = cos_ref[...], sin_ref[...]
    cq = _rms(cq_ref[...].astype(F32), gq_ref[...]).astype(BF16)
    q = _dot(cq, wq_ref[...]) * cos + _dot(cq, wqr_ref[...]) * sin
    q_ref[...] = (q * scale).astype(BF16)
    ckv = _rms(ckv_ref[...].astype(F32), gkv_ref[...]).astype(BF16)
    kr = small_ref[...].astype(BF16)
    k = _dot(ckv, wk_ref[...]) + _dot(kr, pk_ref[...]) * cos + _dot(kr, pkr_ref[...]) * sin
    k_ref[...] = k.astype(BF16)
    v_ref[0] = _values_t(_dot(ckv, wv_ref[...]))


def _mla_prep(cq, ckv, small, cos_t, sin_t, gq, gkv, wq, wqr, wk, wv, pk, pkr, *, seq):
    t = cq.shape[0]
    tm = min(TOKEN_TILE, seq)
    tiles_per_seq = seq // tm
    wfull = MLA_HEAD_PAD * N_HEADS
    row = lambda i: (i, 0)
    pos = lambda i: (i % tiles_per_seq, 0)
    fixed = lambda i: (0, 0)
    return pl.pallas_call(
        functools.partial(_mla_prep_kernel, scale=float((MLA_NOPE + MLA_ROPE) ** -0.5 * LOG2E)),
        grid=(t // tm,),
        in_specs=[pl.BlockSpec((tm, MLA_Q_LORA), row), pl.BlockSpec((tm, MLA_KV_LORA), row),
                  pl.BlockSpec((tm, LANES), row),
                  pl.BlockSpec((tm, wfull), pos), pl.BlockSpec((tm, wfull), pos),
                  pl.BlockSpec((1, MLA_Q_LORA), fixed), pl.BlockSpec((1, MLA_KV_LORA), fixed),
                  pl.BlockSpec((MLA_Q_LORA, wfull), fixed), pl.BlockSpec((MLA_Q_LORA, wfull), fixed),
                  pl.BlockSpec((MLA_KV_LORA, wfull), fixed), pl.BlockSpec((MLA_KV_LORA, ATT_W), fixed),
                  pl.BlockSpec((LANES, wfull), fixed), pl.BlockSpec((LANES, wfull), fixed)],
        out_specs=[pl.BlockSpec((tm, wfull), row), pl.BlockSpec((tm, wfull), row),
                   pl.BlockSpec((1, N_HEADS * V_ROWS, tm), lambda i: (i // tiles_per_seq, 0, i % tiles_per_seq))],
        out_shape=[jax.ShapeDtypeStruct((t, wfull), BF16), jax.ShapeDtypeStruct((t, wfull), BF16),
                   jax.ShapeDtypeStruct((t // seq, N_HEADS * V_ROWS, seq), BF16)],
        compiler_params=_cparams(("parallel",)),
        name="mla_prep",
    )(cq, ckv, small, cos_t, sin_t, gq, gkv, wq, wqr, wk, wv, pk, pkr)


def _merge_kernel(h_ref, g_ref, ya_ref, yb_ref, yc_ref, yd_ref, wg_ref, bg_ref, wb_ref, wo_ref, o_ref):
    h = h_ref[...]
    xn = _rms(h, g_ref[...]).astype(BF16)
    merged = jnp.zeros(h.shape, F32)
    for i, y_ref in enumerate((ya_ref, yb_ref, yc_ref, yd_ref)):
        gate = _sigmoid(_dot(xn, wg_ref[i]) + bg_ref[i])
        merged = merged + gate * _dot(y_ref[...], wb_ref[i])
    o_ref[...] = h + _dot(merged.astype(BF16), wo_ref[...])


def _merge(h, g, ys, wg, bg, wb, wo):
    t, d = h.shape
    tm = min(TOKEN_TILE, t)
    row = lambda i: (i, 0)
    fixed2 = lambda i: (0, 0)
    fixed3 = lambda i: (0, 0, 0)
    return pl.pallas_call(
        _merge_kernel,
        grid=(t // tm,),
        in_specs=[pl.BlockSpec((tm, d), row), pl.BlockSpec((1, d), fixed2)]
        + [pl.BlockSpec((tm, BRANCH_W), row)] * N_BRANCH
        + [pl.BlockSpec((N_BRANCH, d, d), fixed3), pl.BlockSpec((N_BRANCH, 1, d), fixed3),
           pl.BlockSpec((N_BRANCH, BRANCH_W, d), fixed3), pl.BlockSpec((d, d), fixed2)],
        out_specs=pl.BlockSpec((tm, d), row),
        out_shape=jax.ShapeDtypeStruct((t, d), F32),
        compiler_params=_cparams(("parallel",)),
        name="merge",
    )(h, g, *ys, wg, bg, wb, wo)


ROW_TILE = 8


def _store_row_tiles(ref, x):
    m = x.shape[0]
    for j in range(ROW_TILE):
        ref[pl.ds(j, m, stride=ROW_TILE), :] = x[:, j * LANES:(j + 1) * LANES]


def _load_row_tiles(ref):
    m = ref.shape[0] // ROW_TILE
    return jnp.concatenate([ref[pl.ds(j, m, stride=ROW_TILE), :] for j in range(ROW_TILE)], axis=1)


def _route_kernel(h_ref, g_ref, wr_ref, br_ref, hn_ref, ri_ref, rw_ref, cnt_ref, carry_ref):
    i = pl.program_id(0)
    hn = _rms(h_ref[...], g_ref[...])
    _store_row_tiles(hn_ref, hn)
    lg = jnp.dot(hn, wr_ref[...], precision=lax.Precision.HIGHEST, preferred_element_type=F32) + br_ref[...]
    lane = lax.broadcasted_iota(I32, lg.shape, 1)
    big = jnp.int32(4 * LANES)

    def top1(vals):
        mx = jnp.max(vals, axis=1, keepdims=True)
        idx = jnp.min(jnp.where(vals == mx, lane, big), axis=1, keepdims=True)
        return mx, idx

    is_g = (lane >= N_EXPERTS) & (lane < N_EXPERTS + N_GROUPS)
    gmax, gidx = top1(jnp.where(is_g, lg, NEG_INF))
    p_sel = 1.0 / jnp.sum(jnp.where(is_g, jnp.exp(lg - gmax), 0.0), axis=1, keepdims=True)
    lo = (gidx - N_EXPERTS) * EXPERTS_PER_GROUP
    el = jnp.where((lane >= lo) & (lane < lo + EXPERTS_PER_GROUP), lg, NEG_INF)
    l1, i1 = top1(el)
    l2, i2 = top1(jnp.where(lane == i1, NEG_INF, el))
    e = jnp.exp(l2 - l1)
    w1 = p_sel / (1.0 + e)
    w2 = p_sel * e / (1.0 + e)

    @pl.when(i == 0)
    def _():
        carry_ref[...] = jnp.zeros_like(carry_ref)

    onehot = jnp.where((lane == i1) | (lane == i2), 1.0, 0.0)
    carry_ref[...] = carry_ref[...] + jnp.sum(onehot, axis=0, keepdims=True)
    cnt_ref[...] = carry_ref[...]
    ri_ref[...] = jnp.where(lane == 0, i1, jnp.where(lane == 1, i2, 0))
    rw_ref[...] = jnp.where(lane == 0, w1, jnp.where(lane == 1, w2, 0.0))


def _route(h, g, wr, br):
    t, d = h.shape
    tm = min(TOKEN_TILE, t)
    row = lambda i: (i, 0)
    fixed = lambda i: (0, 0)
    return pl.pallas_call(
        _route_kernel,
        grid=(t // tm,),
        in_specs=[pl.BlockSpec((tm, d), row), pl.BlockSpec((1, d), fixed),
                  pl.BlockSpec((d, LANES), fixed), pl.BlockSpec((1, LANES), fixed)],
        out_specs=[pl.BlockSpec((tm * ROW_TILE, LANES), row), pl.BlockSpec((tm, LANES), row),
                   pl.BlockSpec((tm, LANES), row), pl.BlockSpec((1, LANES), fixed)],
        out_shape=[jax.ShapeDtypeStruct((t * ROW_TILE, LANES), F32), jax.ShapeDtypeStruct((t, LANES), I32),
                   jax.ShapeDtypeStruct((t, LANES), F32), jax.ShapeDtypeStruct((1, LANES), F32)],
        scratch_shapes=[pltpu.VMEM((1, LANES), F32)],
        compiler_params=_cparams(("arbitrary",)),
        name="route",
    )(h, g, wr, br)


def _ffn_kernel(blk_e_ref, src_cur_ref, src_nxt_ref, dst_ref, hn_hbm, w13_ref, w2_ref, y_hbm,
                src_s, dst_s, xa, xb, oa, ob, sem_idx, sem_in, sem_out):
    i = pl.program_id(0)
    n = pl.num_programs(0)
    xbufs, obufs = (xa, xb), (oa, ob)
    rows = xa.shape[0] // ROW_TILE

    def to_smem(vmem_ref, smem_ref, sem):
        cp = pltpu.make_async_copy(vmem_ref.at[0, 0], smem_ref, sem)
        cp.start()
        cp.wait()

    def tile(ref, r):
        return ref.at[pl.ds(pl.multiple_of(r * ROW_TILE, ROW_TILE), ROW_TILE)]

    def start_gather(s):
        for r in range(rows):
            pltpu.make_async_copy(tile(hn_hbm, src_s[r]), tile(xbufs[s], r), sem_in.at[s]).start()

    def start_scatter(s):
        for r in range(rows):
            pltpu.make_async_copy(tile(obufs[s], r), tile(y_hbm, dst_s[r]), sem_out.at[s]).start()

    def wait_gather(s):
        pltpu.make_async_copy(hn_hbm.at[pl.ds(0, rows * ROW_TILE)], xbufs[s], sem_in.at[s]).wait()

    def wait_scatter(s):
        pltpu.make_async_copy(obufs[s], y_hbm.at[pl.ds(0, rows * ROW_TILE)], sem_out.at[s]).wait()

    @pl.when(i == 0)
    def _():
        to_smem(src_cur_ref, src_s, sem_idx.at[0])
        start_gather(0)

    def block(s):
        to_smem(src_nxt_ref, src_s, sem_idx.at[0])
        to_smem(dst_ref, dst_s, sem_idx.at[1])

        @pl.when(i + 1 < n)
        def _():
            start_gather(1 - s)

        wait_gather(s)

        @pl.when(i >= 2)
        def _():
            wait_scatter(s)

        x = _load_row_tiles(xbufs[s]).astype(BF16)
        ab = _dot(x, w13_ref[0])
        a, b = ab[:, :EXPERT_FF], ab[:, EXPERT_FF:]
        act = (a * _sigmoid(a) * b).astype(BF16)
        _store_row_tiles(obufs[s], _dot(act, w2_ref[0]))
        start_scatter(s)

        @pl.when(i == n - 1)
        def _():
            wait_scatter(s)
            wait_scatter(1 - s)

    for s in range(2):
        pl.when(i % 2 == s)(functools.partial(block, s))


def _expert_ffn(hn, src_tok, dst_row, blk_e, w13, w2, *, y_rows):
    d = ROW_TILE * LANES
    n_blk = src_tok.shape[0]
    assert n_blk >= 2
    idx_blk = (1, 1, FFN_ROWS)
    tile_buf = pltpu.VMEM((FFN_ROWS * ROW_TILE, LANES), F32)
    grid_spec = pltpu.PrefetchScalarGridSpec(
        num_scalar_prefetch=1,
        grid=(n_blk,),
        in_specs=[pl.BlockSpec(idx_blk, lambda i, be: (i, 0, 0)),
                  pl.BlockSpec(idx_blk, lambda i, be: (jnp.minimum(i + 1, n_blk - 1), 0, 0)),
                  pl.BlockSpec(idx_blk, lambda i, be: (i, 0, 0)),
                  pl.BlockSpec(memory_space=pl.ANY),
                  pl.BlockSpec((1, d, 2 * EXPERT_FF), lambda i, be: (be[i], 0, 0)),
                  pl.BlockSpec((1, EXPERT_FF, d), lambda i, be: (be[i], 0, 0))],
        out_specs=pl.BlockSpec(memory_space=pl.ANY),
        scratch_shapes=[pltpu.SMEM((FFN_ROWS,), I32), pltpu.SMEM((FFN_ROWS,), I32),
                        tile_buf, tile_buf, tile_buf, tile_buf,
                        pltpu.SemaphoreType.DMA((2,)), pltpu.SemaphoreType.DMA((2,)),
                        pltpu.SemaphoreType.DMA((2,))],
    )
    return pl.pallas_call(
        _ffn_kernel,
        grid_spec=grid_spec,
        out_shape=jax.ShapeDtypeStruct((y_rows * ROW_TILE, LANES), F32),
        compiler_params=_cparams(("arbitrary",)),
        name="expert_ffn",
    )(blk_e, src_tok, src_tok, dst_row, hn, w13, w2)


def _final_kernel(h_ref, ya_ref, yb_ref, rw_ref, g_ref, o_ref):
    o_ref[...] = _rms(_moe_combine(h_ref, (ya_ref, yb_ref, rw_ref)), g_ref[...])


def _final_norm(h, y, rw, g):
    t, d = h.shape
    tm = min(TOKEN_TILE, t)
    return pl.pallas_call(
        _final_kernel,
        grid=(t // tm,),
        in_specs=[pl.BlockSpec((tm, d), lambda i: (i, 0))] + _moe_specs(t, tm, d)
        + [pl.BlockSpec((1, d), lambda i: (0, 0))],
        out_specs=pl.BlockSpec((tm, d), lambda i: (i, 0)),
        out_shape=jax.ShapeDtypeStruct((t, d), F32),
        compiler_params=_cparams(("parallel",)),
        name="final_norm",
    )(h, y, y, rw, g)


def _prep_inproj(w_in, b_forget):
    d = w_in.shape[0]
    o = np.cumsum([0, 3 * ATT_W, 3 * ATT_W, N_HEADS, 2 * CONV_CH, MLA_Q_LORA, MLA_KV_LORA, MLA_ROPE]).tolist()
    parts = [w_in[:, o[0]:o[1]], w_in[:, o[1]:o[2]], w_in[:, o[3]:o[4]],
             w_in[:, o[4]:o[5]], w_in[:, o[5]:o[6]], w_in[:, o[6]:o[7]], w_in[:, o[2]:o[3]],
             jnp.zeros((d, C_END - C_SMALL - MLA_ROPE - N_HEADS), F32)]
    w = jnp.concatenate(parts, axis=1).astype(BF16)
    bf = jnp.zeros((1, LANES), F32).at[0, SM_CUM:SM_CUM + N_HEADS].set(b_forget)
    return w, bf


def _rope_placement():
    hp, half = MLA_HEAD_PAD, MLA_ROPE // 2
    pk = np.zeros((LANES, hp * N_HEADS), np.float32)
    pkr = np.zeros((LANES, hp * N_HEADS), np.float32)
    for h in range(N_HEADS):
        for i in range(MLA_ROPE):
            pk[SM_ROPE + i, h * hp + MLA_NOPE + i] = 1.0
        for i in range(half):
            pkr[SM_ROPE + half + i, h * hp + MLA_NOPE + i] = -1.0
            pkr[SM_ROPE + i, h * hp + MLA_NOPE + half + i] = 1.0
    return jnp.asarray(pk, BF16), jnp.asarray(pkr, BF16)


def _prep_mla(wuq, wukv):
    hp, half = MLA_HEAD_PAD, MLA_ROPE // 2
    qd = MLA_NOPE + MLA_ROPE
    zq = jnp.zeros((MLA_Q_LORA, hp - qd), F32)
    wq, wqr, wk, wv = [], [], [], []
    for h in range(N_HEADS):
        nope = wuq[:, h * qd:h * qd + MLA_NOPE]
        pe = wuq[:, h * qd + MLA_NOPE:(h + 1) * qd]
        wq += [nope, pe, zq]
        wqr += [jnp.zeros_like(nope), -pe[:, half:], pe[:, :half], zq]
        kv = wukv[:, h * (MLA_NOPE + MLA_V):(h + 1) * (MLA_NOPE + MLA_V)]
        wk += [kv[:, :MLA_NOPE], jnp.zeros((MLA_KV_LORA, hp - MLA_NOPE), F32)]
        wv += [kv[:, MLA_NOPE:]]
    cat = lambda xs: jnp.concatenate(xs, axis=1).astype(BF16)
    return cat(wq), cat(wqr), cat(wk), cat(wv)


def _rope_tables(seq):
    inv_freq = ROPE_THETA ** (-jnp.arange(0, MLA_ROPE, 2, dtype=F32) / MLA_ROPE)
    ang = jnp.arange(seq, dtype=F32)[:, None] * inv_freq[None, :]
    cos, sin = jnp.cos(ang), jnp.sin(ang)
    one = jnp.ones((seq, MLA_NOPE), F32)
    zero = jnp.zeros((seq, MLA_NOPE), F32)
    pad = jnp.zeros((seq, MLA_HEAD_PAD - MLA_NOPE - MLA_ROPE), F32)
    cos_h = jnp.concatenate([one, cos, cos, pad], axis=1)
    sin_h = jnp.concatenate([zero, sin, sin, pad], axis=1)
    return jnp.tile(cos_h, (1, N_HEADS)), jnp.tile(sin_h, (1, N_HEADS))


def kernel(x, norm1_g, w_in, b_forget, conv_w, conv_b, conv_ln_g, conv_ln_b, mla_gq, mla_wuq, mla_gkv,
           mla_wukv, w_gate, b_gate, w_branch, w_out, norm2_g, w_router_group, b_router_group,
           w_router_expert, b_router_expert, w_exp_gate, w_exp_up, w_exp_down, final_g):
    batch, seq, d = x.shape
    t = batch * seq
    depth = w_in.shape[0]
    assert seq % MOBA_BLOCK == 0 and t % TOKEN_TILE == 0
    cos_t, sin_t = _rope_tables(seq)
    pk, pkr = _rope_placement()
    n_slots = 2 * t
    p_rows = n_slots + N_EXPERTS * FFN_ROWS
    n_blk = p_rows // FFN_ROWS

    h = x.reshape(t, d)
    moe = None
    for l in range(depth):
        w_l, bf_l = _prep_inproj(w_in[l], b_forget[l])
        outs = _inproj(h, norm1_g[l][None], w_l, bf_l, seq=seq, moe=moe)
        if moe is not None:
            h, *outs = outs
        moba, moba_vt, fox, fox_vt, conv_in, cq, ckv, small, cumt = outs
        y_a = _attention("moba", moba, 0, moba, 1, moba_vt, batch=batch, seq=seq)
        y_b = _attention("fox", fox, 0, fox, 1, fox_vt, batch=batch, seq=seq, extra=(small, cumt))
        conv_wp = jnp.zeros((32, CONV_CH), F32).at[:CONV_K].set(conv_w[l])
        conv_p = jnp.zeros((SUBLANES, CONV_CH), F32).at[0].set(conv_b[l]).at[1].set(conv_ln_g[l]).at[2].set(
            conv_ln_b[l])
        y_c = _conv_module(conv_in, conv_wp, conv_p, batch=batch, seq=seq)
        wq, wqr, wk, wv = _prep_mla(mla_wuq[l], mla_wukv[l])
        qd, kd, vd = _mla_prep(cq, ckv, small, cos_t, sin_t, mla_gq[l][None], mla_gkv[l][None],
                               wq, wqr, wk, wv, pk, pkr, seq=seq)
        y_d = _attention("mla", qd, 0, kd, 0, vd, batch=batch, seq=seq)
        h = _merge(h, norm1_g[l][None], (y_a, y_b, y_c, y_d), w_gate[l].astype(BF16), b_gate[l][:, None, :],
                   w_branch[l].astype(BF16), w_out[l].astype(BF16))

        wr = jnp.zeros((d, LANES), F32).at[:, :N_EXPERTS].set(w_router_expert[l]).at[
            :, N_EXPERTS:N_EXPERTS + N_GROUPS].set(w_router_group[l])
        br = jnp.zeros((1, LANES), F32).at[0, :N_EXPERTS].set(b_router_expert[l]).at[
            0, N_EXPERTS:N_EXPERTS + N_GROUPS].set(b_router_group[l])
        hn, ri, rw, cnt = _route(h, norm2_g[l][None], wr, br)
        counts = cnt[0, :N_EXPERTS].astype(I32)
        starts = jnp.cumsum(counts) - counts
        pcounts = ((counts + FFN_ROWS - 1) // FFN_ROWS) * FFN_ROWS
        pends = jnp.cumsum(pcounts)
        pstarts = pends - pcounts
        order = jnp.argsort(ri[:, 0:2].T.reshape(-1)).astype(I32)
        blk_start = jnp.arange(n_blk, dtype=I32) * FFN_ROWS
        blk_e = jnp.minimum(jnp.sum(blk_start[:, None] >= pends[None, :], axis=1), N_EXPERTS - 1).astype(I32)
        pos = jnp.arange(p_rows, dtype=I32)
        e_p = jnp.repeat(blk_e, FFN_ROWS)
        off = pos - pstarts[e_p]
        real = off < counts[e_p]
        before = starts[e_p] + jnp.minimum(off, counts[e_p])
        slot = order[jnp.minimum(before, n_slots - 1)]
        src_tok = jnp.where(real, slot % t, 0)
        dst_row = jnp.where(real, slot, n_slots + pos - before)
        w13 = jnp.concatenate([w_exp_gate[l], w_exp_up[l]], axis=-1).astype(BF16)
        y = _expert_ffn(hn, src_tok.reshape(n_blk, 1, FFN_ROWS), dst_row.reshape(n_blk, 1, FFN_ROWS), blk_e,
                        w13, w_exp_down[l].astype(BF16), y_rows=p_rows)
        moe = (y, rw)
    return _final_norm(h, *moe, final_g[None]).reshape(batch, seq, d)
```

```python
import functools

import numpy as np
import jax
import jax.numpy as jnp
from jax import lax
from jax.experimental import pallas as pl
from jax.experimental.pallas import tpu as pltpu

F32 = jnp.float32
BF16 = jnp.bfloat16
I32 = jnp.int32

HEAD_DIM = 64
N_HEADS = 4
ATT_W = N_HEADS * HEAD_DIM
MOBA_BLOCK = 256
MOBA_TOPK = 3
CONV_CH = 256
CONV_K = 31
MLA_Q_LORA = 256
MLA_KV_LORA = 128
MLA_NOPE = 64
MLA_ROPE = 32
MLA_V = 64
MLA_HEAD_PAD = 128
ROPE_THETA = 10000.0
BRANCH_W = 256
N_BRANCH = 4
N_GROUPS = 4
EXPERTS_PER_GROUP = 8
N_EXPERTS = N_GROUPS * EXPERTS_PER_GROUP
EXPERT_FF = 256
NORM_EPS = 1e-6
NEG_INF = -1e30
LOG2E = 1.4426950408889634

LANES = 128
SUBLANES = 8
VMEM_LIMIT = 56 * 1024 * 1024

TOKEN_TILE = 512
ATT_TK = 256
V_ROWS = 80
ATT_UNROLL = 4
CONV_CHUNK = 128
FFN_ROWS = 256

C_MOBA, C_FOX, C_CONV, C_CQ, C_CKV, C_SMALL, C_END = 0, 768, 1536, 2048, 2304, 2432, 2560
SM_ROPE = 0
SM_CUM = 32


def _cparams(sem):
    return pltpu.CompilerParams(dimension_semantics=sem, vmem_limit_bytes=VMEM_LIMIT)


def _rms(x, g):
    return x * lax.rsqrt(jnp.mean(x * x, axis=-1, keepdims=True) + NORM_EPS) * g


def _sigmoid(x):
    return 1.0 / (1.0 + jnp.exp(-x))


def _dot(a, b):
    return jnp.dot(a, b, preferred_element_type=F32)


def _values_t(v):
    vt = v.T
    ones = jnp.ones((V_ROWS - HEAD_DIM, vt.shape[1]), F32)
    parts = [x for h in range(N_HEADS) for x in (vt[h * HEAD_DIM:(h + 1) * HEAD_DIM, :], ones)]
    return jnp.concatenate(parts, axis=0).astype(BF16)


def _moe_combine(h_ref, moe_refs):
    ya_ref, yb_ref, rw_ref = moe_refs
    rw = rw_ref[...]
    return h_ref[...] + rw[:, 0:1] * _load_row_tiles(ya_ref) + rw[:, 1:2] * _load_row_tiles(yb_ref)


def _inproj_kernel(*refs, tiles_per_seq, combine):
    if combine:
        h_ref, *moe_refs, g_ref, w_ref, bf_ref, hout_ref = refs[:8]
        refs = refs[8:]
        h = _moe_combine(h_ref, moe_refs)
        hout_ref[...] = h
    else:
        h_ref, g_ref, w_ref, bf_ref = refs[:4]
        refs = refs[4:]
        h = h_ref[...]
    moba_ref, mvt_ref, fox_ref, fvt_ref, conv_ref, cq_ref, ckv_ref, small_ref, cumt_ref, carry_ref = refs
    i = pl.program_id(0)
    tm = h_ref.shape[0]
    xn = _rms(h, g_ref[...]).astype(BF16)

    def proj(a, b):
        return _dot(xn, w_ref[:, a:b])

    for qk_ref, vt_ref, c0 in ((moba_ref, mvt_ref, C_MOBA), (fox_ref, fvt_ref, C_FOX)):
        qk_ref[:, :ATT_W] = (proj(c0, c0 + ATT_W) * (HEAD_DIM ** -0.5 * LOG2E)).astype(BF16)
        qk_ref[:, ATT_W:] = proj(c0 + ATT_W, c0 + 2 * ATT_W).astype(BF16)
        vt_ref[0] = _values_t(proj(c0 + 2 * ATT_W, c0 + 3 * ATT_W))
    conv_ref[...] = proj(C_CONV, C_CQ).astype(BF16)
    cq_ref[...] = proj(C_CQ, C_CKV).astype(BF16)
    ckv_ref[...] = proj(C_CKV, C_SMALL).astype(BF16)
    sm = proj(C_SMALL, C_END)

    f = sm + bf_ref[...]
    c = jnp.minimum(f, 0.0) - jnp.log(1.0 + jnp.exp(-jnp.abs(f)))
    row = lax.broadcasted_iota(I32, c.shape, 0)
    sh = 1
    while sh < tm:
        c = c + jnp.where(row >= sh, pltpu.roll(c, sh, 0), 0.0)
        sh *= 2

    @pl.when(i % tiles_per_seq == 0)
    def _():
        carry_ref[...] = jnp.zeros_like(carry_ref)

    c = c + carry_ref[...]
    carry_ref[...] = c[tm - 1:tm, :]
    lane = lax.broadcasted_iota(I32, c.shape, 1)
    small_ref[...] = jnp.where((lane >= SM_CUM) & (lane < SM_CUM + N_HEADS), c, sm)
    cumt_ref[0] = c.T[SM_CUM:SM_CUM + SUBLANES, :]


def _moe_specs(t, tm, d):
    assert d == ROW_TILE * LANES
    return [pl.BlockSpec((tm * ROW_TILE, LANES), lambda i: (i, 0)),
            pl.BlockSpec((tm * ROW_TILE, LANES), lambda i: (i + t // tm, 0)),
            pl.BlockSpec((tm, LANES), lambda i: (i, 0))]


def _inproj(h, g, w, bf, *, seq, moe=None):
    t, d = h.shape
    tm = min(TOKEN_TILE, seq)
    tiles_per_seq = seq // tm
    nb = t // seq
    row = lambda i: (i, 0)
    fixed = lambda i: (0, 0)
    seq_t = lambda i: (i // tiles_per_seq, 0, i % tiles_per_seq)
    tok = lambda wdt, dt: (pl.BlockSpec((tm, wdt), row), jax.ShapeDtypeStruct((t, wdt), dt))
    chan = lambda c, dt: (pl.BlockSpec((1, c, tm), seq_t), jax.ShapeDtypeStruct((nb, c, seq), dt))
    outs = [tok(2 * ATT_W, BF16), chan(N_HEADS * V_ROWS, BF16), tok(2 * ATT_W, BF16), chan(N_HEADS * V_ROWS, BF16),
            tok(C_CQ - C_CONV, BF16), tok(C_CKV - C_CQ, BF16), tok(C_SMALL - C_CKV, BF16),
            tok(C_END - C_SMALL, F32), chan(SUBLANES, F32)]
    if moe is not None:
        outs = [tok(d, F32)] + outs
    return pl.pallas_call(
        functools.partial(_inproj_kernel, tiles_per_seq=tiles_per_seq, combine=moe is not None),
        grid=(t // tm,),
        in_specs=[pl.BlockSpec((tm, d), row)] + (_moe_specs(t, tm, d) if moe is not None else [])
        + [pl.BlockSpec((1, d), fixed), pl.BlockSpec((d, C_END), fixed), pl.BlockSpec((1, LANES), fixed)],
        out_specs=[o[0] for o in outs],
        out_shape=[o[1] for o in outs],
        scratch_shapes=[pltpu.VMEM((1, LANES), F32)],
        compiler_params=_cparams(("arbitrary",)),
        name="inproj",
    )(h, *((moe[0], moe[0], moe[1]) if moe is not None else ()), g, w, bf)


def _attn_kernel(*refs, mode, tq, tk, nblk, n_sel):
    qs_ref, ua_ref, ub_ref, mba_ref, mbb_ref, m_ref, acc_ref = refs[-7:]
    refs = refs[:-7]
    if mode == "fox":
        q_ref, k_ref, vt_ref, small_ref, cumt_ref, o_ref, csb_ref = refs
    elif mode == "moba":
        q_ref, k_ref, vt_ref, o_ref, kmean_ref, selb_ref = refs
    else:
        q_ref, k_ref, vt_ref, o_ref = refs
    i = pl.program_id(1)
    nh = N_HEADS
    rows = nh * tq
    seq = k_ref.shape[1]
    n_full = (i * tq) // tk
    q0 = i * tq

    q = q_ref[0]
    if mode == "mla":
        for h in range(nh):
            qs_ref[:, h * tq:(h + 1) * tq] = q[:, h * MLA_HEAD_PAD:(h + 1) * MLA_HEAD_PAD].astype(F32).T.astype(BF16)
    else:
        qt = q.astype(F32).T.astype(BF16)
        sub_h = lax.shift_right_logical(lax.broadcasted_iota(I32, (ATT_W, tq), 0), 6)
        for h in range(nh):
            qs_ref[:, h * tq:(h + 1) * tq] = jnp.where(sub_h == h, qt, jnp.zeros_like(qt))

    r_in = lax.broadcasted_iota(I32, (tk, rows), 1) & (tq - 1)
    c_in = lax.broadcasted_iota(I32, (tk, rows), 0)

    def per_head_rows(x):
        return jnp.concatenate(
            [jnp.broadcast_to(x[:, h * tq:(h + 1) * tq], (V_ROWS, tq)) for h in range(nh)], axis=0)

    if mode == "fox":
        @pl.when(i == 0)
        def _():
            def fill(c, _):
                r0 = pl.multiple_of(c * tk, tk)
                cs = small_ref[0, pl.ds(r0, tk), :]
                for h in range(nh):
                    csb_ref[pl.ds(r0, tk), h * LANES:(h + 1) * LANES] = jnp.broadcast_to(
                        -LOG2E * cs[:, SM_CUM + h:SM_CUM + h + 1], (tk, LANES))
                return 0
            lax.fori_loop(0, seq // tk, fill, 0)

        ct = cumt_ref[0] * LOG2E
        c_t = jnp.concatenate([ct[h:h + 1, :] for h in range(nh)], axis=1)
    if mode == "moba":
        slopes = [LOG2E * 2.0 ** (-8.0 * (h + 1) / nh) for h in range(nh)]
        slope_row = jnp.concatenate([jnp.full((1, tq), s, F32) for s in slopes], axis=1)
        alibi_key = slope_row * c_in.astype(F32)
        r_row = (lax.broadcasted_iota(I32, (1, rows), 1) & (tq - 1)).astype(F32)
        nbp = kmean_ref.shape[0]

        @pl.when(i == 0)
        def _():
            kmean_ref[...] = jnp.zeros_like(kmean_ref)
            for n in range(nblk):
                kb = k_ref[0, n * MOBA_BLOCK:(n + 1) * MOBA_BLOCK, :].astype(F32)
                kmean_ref[n:n + 1, :] = jnp.mean(kb, axis=0, keepdims=True)

        km = kmean_ref[...]
        km_hi = km.astype(BF16)
        km_lo = (km - km_hi.astype(F32)).astype(BF16)
        cur = (i * tq) // MOBA_BLOCK
        qs = qs_ref[...]
        gate = _dot(km_hi, qs) + _dot(km_lo, qs)
        n_iota = lax.broadcasted_iota(I32, gate.shape, 0)
        gate = jnp.where(n_iota < cur, gate, NEG_INF)
        rank = jnp.zeros(gate.shape, I32)
        for n2 in range(nblk):
            gb = gate[n2:n2 + 1, :]
            ahead = jnp.where(gb > gate, 1, jnp.where((gb == gate) & (n_iota > n2), 1, 0))
            rank = rank + ahead
        selb_ref[...] = jnp.where((rank < n_sel) & (n_iota < cur), 0.0, NEG_INF)

    def produce(u_ref, mb_ref, j, diagonal=False):
        k0 = pl.multiple_of(j * tk, tk)
        kj = k_ref[0, pl.ds(k0, tk), :]
        if mode == "mla":
            u = jnp.concatenate(
                [_dot(kj[:, h * MLA_HEAD_PAD:(h + 1) * MLA_HEAD_PAD], qs_ref[:, h * tq:(h + 1) * tq])
                 for h in range(nh)], axis=1)
        else:
            u = _dot(kj, qs_ref[...])
        if mode == "fox":
            cs = csb_ref[pl.ds(k0, tk), :]
            u = u + jnp.concatenate([cs[:, h * LANES:(h + 1) * LANES] for h in range(nh)
                                     for _ in range(tq // LANES)], axis=1)
        if mode == "moba":
            u = u + alibi_key
        if diagonal:
            u = jnp.where(c_in + j * tk <= r_in + q0, u, NEG_INF)
        u_ref[...] = u
        mb_ref[...] = jnp.max(u, axis=0, keepdims=True)

    def consume(u_ref, mb_ref, j, diagonal=False):
        vtj = vt_ref[0, :, pl.ds(pl.multiple_of(j * tk, tk), tk)]
        m = m_ref[...]
        m_blk = mb_ref[...]
        if mode == "fox":
            row = c_t
        if mode == "moba":
            row = slope_row * ((j * tk - q0).astype(F32) - r_row)
            if not diagonal:
                row = row + selb_ref[pl.ds(j, 1), :]
        if mode != "mla":
            m_blk = m_blk + row
        m_new = jnp.maximum(m, m_blk)
        alpha = jnp.exp2(m - m_new)
        shift = m_new - row if mode != "mla" else m_new
        os_ = []
        for h in range(nh):
            cols = slice(h * tq, (h + 1) * tq)
            p = jnp.exp2(u_ref[:, cols] - shift[:, cols])
            os_.append(_dot(vtj[h * V_ROWS:(h + 1) * V_ROWS, :], p.astype(BF16)))
        m_ref[...] = m_new
        acc_ref[...] = per_head_rows(alpha) * acc_ref[...] + jnp.concatenate(os_, axis=0)

    m_ref[...] = jnp.full(m_ref.shape, NEG_INF, F32)
    acc_ref[...] = jnp.zeros(acc_ref.shape, F32)
    produce(ub_ref, mbb_ref, n_full, True)
    produce(ua_ref, mba_ref, 0)
    consume(ub_ref, mbb_ref, n_full, True)

    bufs = ((ua_ref, mba_ref), (ub_ref, mbb_ref))

    def trip(t, _):
        for k in range(ATT_UNROLL):
            produce(*bufs[(k + 1) % 2], ATT_UNROLL * t + k + 1)
            consume(*bufs[k % 2], ATT_UNROLL * t + k)
        return 0

    lax.fori_loop(0, n_full // ATT_UNROLL, trip, 0)
    base = (n_full // ATT_UNROLL) * ATT_UNROLL
    for k in range(ATT_UNROLL - 1):
        @pl.when(n_full - base > k)
        def _():
            if k + 1 < ATT_UNROLL - 1:
                produce(*bufs[(k + 1) % 2], jnp.minimum(base + k + 1, n_full))
            consume(*bufs[k % 2], base + k)

    acc = acc_ref[...]
    out_t = jnp.concatenate(
        [acc[h * V_ROWS:h * V_ROWS + HEAD_DIM, :] * (1.0 / acc[h * V_ROWS + HEAD_DIM:h * V_ROWS + HEAD_DIM + 1, :])
         for h in range(nh)], axis=0)
    o_ref[0] = out_t.T.astype(BF16)


def _attention(mode, q_arr, q_blk, k_arr, k_blk, vt_arr, *, batch, seq, extra=()):
    tq = tk = min(ATT_TK, seq)
    assert tq % LANES == 0 and seq % tk == 0
    dq = MLA_HEAD_PAD * N_HEADS if mode == "mla" else ATT_W
    nblk = seq // MOBA_BLOCK
    rows = N_HEADS * tq
    r3 = lambda a: a.reshape(batch, seq, a.shape[-1])
    in_specs = [pl.BlockSpec((1, tq, dq), lambda b, i: (b, i, q_blk)),
                pl.BlockSpec((1, seq, dq), lambda b, i: (b, 0, k_blk)),
                pl.BlockSpec((1, N_HEADS * V_ROWS, seq), lambda b, i: (b, 0, 0))]
    args = [r3(q_arr), r3(k_arr), vt_arr]
    scratch = []
    if mode == "fox":
        small, cumt = extra
        in_specs += [pl.BlockSpec((1, seq, LANES), lambda b, i: (b, 0, 0)),
                     pl.BlockSpec((1, SUBLANES, tq), lambda b, i: (b, 0, i))]
        args += [r3(small), cumt]
        scratch = [pltpu.VMEM((seq, N_HEADS * LANES), F32)]
    if mode == "moba":
        nbp = -(-nblk // SUBLANES) * SUBLANES
        scratch = [pltpu.VMEM((nbp, ATT_W), F32), pltpu.VMEM((nbp, rows), F32)]
    scratch += [pltpu.VMEM((dq // N_HEADS if mode == "mla" else ATT_W, rows), BF16),
                pltpu.VMEM((tk, rows), F32), pltpu.VMEM((tk, rows), F32),
                pltpu.VMEM((1, rows), F32), pltpu.VMEM((1, rows), F32),
                pltpu.VMEM((1, rows), F32),
                pltpu.VMEM((N_HEADS * V_ROWS, tq), F32)]
    out = pl.pallas_call(
        functools.partial(_attn_kernel, mode=mode, tq=tq, tk=tk, nblk=nblk, n_sel=min(MOBA_TOPK, nblk - 1)),
        grid=(batch, seq // tq),
        in_specs=in_specs,
        out_specs=pl.BlockSpec((1, tq, ATT_W), lambda b, i: (b, i, 0)),
        out_shape=jax.ShapeDtypeStruct((batch, seq, ATT_W), BF16),
        scratch_shapes=scratch,
        compiler_params=_cparams(("parallel", "arbitrary")),
        name="attn_" + mode,
    )(*args)
    return out.reshape(batch * seq, ATT_W)


def _conv_kernel(x_ref, w_ref, p_ref, o_ref, u_ref, *, chunk):
    seq = x_ref.shape[1]
    halo = 32
    u_ref[0:halo, :] = jnp.zeros((halo, CONV_CH), F32)

    def glu(c, _):
        r0 = pl.multiple_of(c * chunk, chunk)
        x = x_ref[0, pl.ds(r0, chunk), :].astype(F32)
        u_ref[pl.ds(halo + r0, chunk), :] = x[:, :CONV_CH] * _sigmoid(x[:, CONV_CH:])
        return 0

    lax.fori_loop(0, seq // chunk, glu, 0)
    bias, ln_g, ln_b = p_ref[0:1, :], p_ref[1:2, :], p_ref[2:3, :]

    def conv(c, _):
        r0 = pl.multiple_of(c * chunk, chunk)
        n = chunk + halo
        win = u_ref[pl.ds(r0, n), :]
        acc = jnp.zeros((chunk, CONV_CH), F32)
        for b in range(SUBLANES):
            shifted = win if b == 0 else pltpu.roll(win, n - b, 0)
            for a in range(halo // SUBLANES + 1):
                k = SUBLANES * a + b - (halo - (CONV_K - 1))
                if 0 <= k < CONV_K:
                    acc = acc + w_ref[k:k + 1, :] * shifted[SUBLANES * a:SUBLANES * a + chunk, :]
        acc = acc + bias
        mu = jnp.mean(acc, axis=-1, keepdims=True)
        xc = acc - mu
        var = jnp.mean(xc * xc, axis=-1, keepdims=True)
        y = xc * lax.rsqrt(var + NORM_EPS) * ln_g + ln_b
        o_ref[0, pl.ds(r0, chunk), :] = (y * _sigmoid(y)).astype(BF16)
        return 0

    lax.fori_loop(0, seq // chunk, conv, 0)


def _conv_module(conv_in, w, p, *, batch, seq):
    chunk = min(CONV_CHUNK, seq)
    out = pl.pallas_call(
        functools.partial(_conv_kernel, chunk=chunk),
        grid=(batch,),
        in_specs=[pl.BlockSpec((1, seq, 2 * CONV_CH), lambda b: (b, 0, 0)),
                  pl.BlockSpec((32, CONV_CH), lambda b: (0, 0)),
                  pl.BlockSpec((SUBLANES, CONV_CH), lambda b: (0, 0))],
        out_specs=pl.BlockSpec((1, seq, CONV_CH), lambda b: (b, 0, 0)),
        out_shape=jax.ShapeDtypeStruct((batch, seq, CONV_CH), BF16),
        scratch_shapes=[pltpu.VMEM((seq + 32, CONV_CH), F32)],
        compiler_params=_cparams(("parallel",)),
        name="conv",
    )(conv_in.reshape(batch, seq, 2 * CONV_CH), w, p)
    return out.reshape(batch * seq, CONV_CH)


def _mla_prep_kernel(cq_ref, ckv_ref, small_ref, cos_ref, sin_ref, gq_ref, gkv_ref, wq_ref, wqr_ref,
                     wk_ref, wv_ref, pk_ref, pkr_ref, q_ref, k_ref, v_ref, *, scale):
    cos, sin = cos_ref[...], sin_ref[...]
    cq = _rms(cq_ref[...].astype(F32), gq_ref[...]).astype(BF16)
    q = _dot(cq, wq_ref[...]) * cos + _dot(cq, wqr_ref[...]) * sin
    q_ref[...] = (q * scale).astype(BF16)
    ckv = _rms(ckv_ref[...].astype(F32), gkv_ref[...]).astype(BF16)
    kr = small_ref[...].astype(BF16)
    k = _dot(ckv, wk_ref[...]) + _dot(kr, pk_ref[...]) * cos + _dot(kr, pkr_ref[...]) * sin
    k_ref[...] = k.astype(BF16)
    v_ref[0] = _values_t(_dot(ckv, wv_ref[...]))


def _mla_prep(cq, ckv, small, cos_t, sin_t, gq, gkv, wq, wqr, wk, wv, pk, pkr, *, seq):
    t = cq.shape[0]
    tm = min(TOKEN_TILE, seq)
    tiles_per_seq = seq // tm
    wfull = MLA_HEAD_PAD * N_HEADS
    row = lambda i: (i, 0)
    pos = lambda i: (i % tiles_per_seq, 0)
    fixed = lambda i: (0, 0)
    return pl.pallas_call(
        functools.partial(_mla_prep_kernel, scale=float((MLA_NOPE + MLA_ROPE) ** -0.5 * LOG2E)),
        grid=(t // tm,),
        in_specs=[pl.BlockSpec((tm, MLA_Q_LORA), row), pl.BlockSpec((tm, MLA_KV_LORA), row),
                  pl.BlockSpec((tm, LANES), row),
                  pl.BlockSpec((tm, wfull), pos), pl.BlockSpec((tm, wfull), pos),
                  pl.BlockSpec((1, MLA_Q_LORA), fixed), pl.BlockSpec((1, MLA_KV_LORA), fixed),
                  pl.BlockSpec((MLA_Q_LORA, wfull), fixed), pl.BlockSpec((MLA_Q_LORA, wfull), fixed),
                  pl.BlockSpec((MLA_KV_LORA, wfull), fixed), pl.BlockSpec((MLA_KV_LORA, ATT_W), fixed),
                  pl.BlockSpec((LANES, wfull), fixed), pl.BlockSpec((LANES, wfull), fixed)],
        out_specs=[pl.BlockSpec((tm, wfull), row), pl.BlockSpec((tm, wfull), row),
                   pl.BlockSpec((1, N_HEADS * V_ROWS, tm), lambda i: (i // tiles_per_seq, 0, i % tiles_per_seq))],
        out_shape=[jax.ShapeDtypeStruct((t, wfull), BF16), jax.ShapeDtypeStruct((t, wfull), BF16),
                   jax.ShapeDtypeStruct((t // seq, N_HEADS * V_ROWS, seq), BF16)],
        compiler_params=_cparams(("parallel",)),
        name="mla_prep",
    )(cq, ckv, small, cos_t, sin_t, gq, gkv, wq, wqr, wk, wv, pk, pkr)


def _merge_kernel(h_ref, g_ref, ya_ref, yb_ref, yc_ref, yd_ref, wg_ref, bg_ref, wb_ref, wo_ref, o_ref):
    h = h_ref[...]
    xn = _rms(h, g_ref[...]).astype(BF16)
    merged = jnp.zeros(h.shape, F32)
    for i, y_ref in enumerate((ya_ref, yb_ref, yc_ref, yd_ref)):
        gate = _sigmoid(_dot(xn, wg_ref[i]) + bg_ref[i])
        merged = merged + gate * _dot(y_ref[...], wb_ref[i])
    o_ref[...] = h + _dot(merged.astype(BF16), wo_ref[...])


def _merge(h, g, ys, wg, bg, wb, wo):
    t, d = h.shape
    tm = min(TOKEN_TILE, t)
    row = lambda i: (i, 0)
    fixed2 = lambda i: (0, 0)
    fixed3 = lambda i: (0, 0, 0)
    return pl.pallas_call(
        _merge_kernel,
        grid=(t // tm,),
        in_specs=[pl.BlockSpec((tm, d), row), pl.BlockSpec((1, d), fixed2)]
        + [pl.BlockSpec((tm, BRANCH_W), row)] * N_BRANCH
        + [pl.BlockSpec((N_BRANCH, d, d), fixed3), pl.BlockSpec((N_BRANCH, 1, d), fixed3),
           pl.BlockSpec((N_BRANCH, BRANCH_W, d), fixed3), pl.BlockSpec((d, d), fixed2)],
        out_specs=pl.BlockSpec((tm, d), row),
        out_shape=jax.ShapeDtypeStruct((t, d), F32),
        compiler_params=_cparams(("parallel",)),
        name="merge",
    )(h, g, *ys, wg, bg, wb, wo)


ROW_TILE = 8


def _store_row_tiles(ref, x):
    m = x.shape[0]
    for j in range(ROW_TILE):
        ref[pl.ds(j, m, stride=ROW_TILE), :] = x[:, j * LANES:(j + 1) * LANES]


def _load_row_tiles(ref):
    m = ref.shape[0] // ROW_TILE
    return jnp.concatenate([ref[pl.ds(j, m, stride=ROW_TILE), :] for j in range(ROW_TILE)], axis=1)


def _route_kernel(h_ref, g_ref, wr_ref, br_ref, hn_ref, ri_ref, rw_ref, cnt_ref, carry_ref):
    i = pl.program_id(0)
    hn = _rms(h_ref[...], g_ref[...])
    _store_row_tiles(hn_ref, hn)
    lg = jnp.dot(hn, wr_ref[...], precision=lax.Precision.HIGHEST, preferred_element_type=F32) + br_ref[...]
    lane = lax.broadcasted_iota(I32, lg.shape, 1)
    big = jnp.int32(4 * LANES)

    def top1(vals):
        mx = jnp.max(vals, axis=1, keepdims=True)
        idx = jnp.min(jnp.where(vals == mx, lane, big), axis=1, keepdims=True)
        return mx, idx

    is_g = (lane >= N_EXPERTS) & (lane < N_EXPERTS + N_GROUPS)
    gmax, gidx = top1(jnp.where(is_g, lg, NEG_INF))
    p_sel = 1.0 / jnp.sum(jnp.where(is_g, jnp.exp(lg - gmax), 0.0), axis=1, keepdims=True)
    lo = (gidx - N_EXPERTS) * EXPERTS_PER_GROUP
    el = jnp.where((lane >= lo) & (lane < lo + EXPERTS_PER_GROUP), lg, NEG_INF)
    l1, i1 = top1(el)
    l2, i2 = top1(jnp.where(lane == i1, NEG_INF, el))
    e = jnp.exp(l2 - l1)
    w1 = p_sel / (1.0 + e)
    w2 = p_sel * e / (1.0 + e)

    @pl.when(i == 0)
    def _():
        carry_ref[...] = jnp.zeros_like(carry_ref)

    onehot = jnp.where((lane == i1) | (lane == i2), 1.0, 0.0)
    carry_ref[...] = carry_ref[...] + jnp.sum(onehot, axis=0, keepdims=True)
    cnt_ref[...] = carry_ref[...]
    ri_ref[...] = jnp.where(lane == 0, i1, jnp.where(lane == 1, i2, 0))
    rw_ref[...] = jnp.where(lane == 0, w1, jnp.where(lane == 1, w2, 0.0))


def _route(h, g, wr, br):
    t, d = h.shape
    tm = min(TOKEN_TILE, t)
    row = lambda i: (i, 0)
    fixed = lambda i: (0, 0)
    return pl.pallas_call(
        _route_kernel,
        grid=(t // tm,),
        in_specs=[pl.BlockSpec((tm, d), row), pl.BlockSpec((1, d), fixed),
                  pl.BlockSpec((d, LANES), fixed), pl.BlockSpec((1, LANES), fixed)],
        out_specs=[pl.BlockSpec((tm * ROW_TILE, LANES), row), pl.BlockSpec((tm, LANES), row),
                   pl.BlockSpec((tm, LANES), row), pl.BlockSpec((1, LANES), fixed)],
        out_shape=[jax.ShapeDtypeStruct((t * ROW_TILE, LANES), F32), jax.ShapeDtypeStruct((t, LANES), I32),
                   jax.ShapeDtypeStruct((t, LANES), F32), jax.ShapeDtypeStruct((1, LANES), F32)],
        scratch_shapes=[pltpu.VMEM((1, LANES), F32)],
        compiler_params=_cparams(("arbitrary",)),
        name="route",
    )(h, g, wr, br)


def _ffn_kernel(blk_e_ref, src0_ref, src1_ref, src2_ref, dst0_ref, dst1_ref, hn_hbm, w13_ref, w2_ref, y_hbm,
                src_s, dst_s, xa, xb, oa, ob, sem_idx, sem_in, sem_out):
    i = pl.program_id(0)
    n = pl.num_programs(0)
    xbufs, obufs = (xa, xb), (oa, ob)
    rows = xa.shape[0] // ROW_TILE

    def src_copy(vmem_ref, p):
        return pltpu.make_async_copy(vmem_ref.at[0, 0], src_s.at[p], sem_idx.at[0])

    def dst_copy(vmem_ref, p):
        return pltpu.make_async_copy(vmem_ref.at[0, 0], dst_s.at[p], sem_idx.at[1])

    def tile(ref, r):
        return ref.at[pl.ds(pl.multiple_of(r * ROW_TILE, ROW_TILE), ROW_TILE)]

    def start_gather(s):
        for r in range(rows):
            pltpu.make_async_copy(tile(hn_hbm, src_s[s, r]), tile(xbufs[s], r), sem_in.at[s]).start()

    def start_scatter(s):
        for r in range(rows):
            pltpu.make_async_copy(tile(obufs[s], r), tile(y_hbm, dst_s[s, r]), sem_out.at[s]).start()

    def wait_gather(s):
        pltpu.make_async_copy(hn_hbm.at[pl.ds(0, rows * ROW_TILE)], xbufs[s], sem_in.at[s]).wait()

    def wait_scatter(s):
        pltpu.make_async_copy(obufs[s], y_hbm.at[pl.ds(0, rows * ROW_TILE)], sem_out.at[s]).wait()

    @pl.when(i == 0)
    def _():
        first = src_copy(src0_ref, 0)
        first.start()
        first.wait()
        start_gather(0)
        second, third = src_copy(src1_ref, 1), dst_copy(dst0_ref, 0)
        second.start()
        third.start()
        second.wait()
        third.wait()

    def block(s):
        @pl.when(i + 1 < n)
        def _():
            start_gather(1 - s)

        nxt_src, nxt_dst = src_copy(src2_ref, s), dst_copy(dst1_ref, 1 - s)
        nxt_src.start()
        nxt_dst.start()
        wait_gather(s)

        @pl.when(i >= 2)
        def _():
            wait_scatter(s)

        x = _load_row_tiles(xbufs[s]).astype(BF16)
        ab = _dot(x, w13_ref[0])
        a, b = ab[:, :EXPERT_FF], ab[:, EXPERT_FF:]
        act = (a * _sigmoid(a) * b).astype(BF16)
        _store_row_tiles(obufs[s], _dot(act, w2_ref[0]))
        start_scatter(s)

        nxt_src.wait()
        nxt_dst.wait()

        @pl.when(i == n - 1)
        def _():
            wait_scatter(s)
            wait_scatter(1 - s)

    for s in range(2):
        pl.when(i % 2 == s)(functools.partial(block, s))


def _expert_ffn(hn, src_tok, dst_row, blk_e, w13, w2, *, y_rows):
    d = ROW_TILE * LANES
    n_blk = src_tok.shape[0]
    assert n_blk >= 2
    idx_blk = (1, 1, FFN_ROWS)
    tile_buf = pltpu.VMEM((FFN_ROWS * ROW_TILE, LANES), F32)
    grid_spec = pltpu.PrefetchScalarGridSpec(
        num_scalar_prefetch=1,
        grid=(n_blk,),
        in_specs=[pl.BlockSpec(idx_blk, lambda i, be: (i, 0, 0)),
                  pl.BlockSpec(idx_blk, lambda i, be: (jnp.minimum(i + 1, n_blk - 1), 0, 0)),
                  pl.BlockSpec(idx_blk, lambda i, be: (jnp.minimum(i + 2, n_blk - 1), 0, 0)),
                  pl.BlockSpec(idx_blk, lambda i, be: (i, 0, 0)),
                  pl.BlockSpec(idx_blk, lambda i, be: (jnp.minimum(i + 1, n_blk - 1), 0, 0)),
                  pl.BlockSpec(memory_space=pl.ANY),
                  pl.BlockSpec((1, d, 2 * EXPERT_FF), lambda i, be: (be[i], 0, 0)),
                  pl.BlockSpec((1, EXPERT_FF, d), lambda i, be: (be[i], 0, 0))],
        out_specs=pl.BlockSpec(memory_space=pl.ANY),
        scratch_shapes=[pltpu.SMEM((2, FFN_ROWS), I32), pltpu.SMEM((2, FFN_ROWS), I32),
                        tile_buf, tile_buf, tile_buf, tile_buf,
                        pltpu.SemaphoreType.DMA((2,)), pltpu.SemaphoreType.DMA((2,)),
                        pltpu.SemaphoreType.DMA((2,))],
    )
    return pl.pallas_call(
        _ffn_kernel,
        grid_spec=grid_spec,
        out_shape=jax.ShapeDtypeStruct((y_rows * ROW_TILE, LANES), F32),
        compiler_params=_cparams(("arbitrary",)),
        name="expert_ffn",
    )(blk_e, src_tok, src_tok, src_tok, dst_row, dst_row, hn, w13, w2)


def _final_kernel(h_ref, ya_ref, yb_ref, rw_ref, g_ref, o_ref):
    o_ref[...] = _rms(_moe_combine(h_ref, (ya_ref, yb_ref, rw_ref)), g_ref[...])


def _final_norm(h, y, rw, g):
    t, d = h.shape
    tm = min(TOKEN_TILE, t)
    return pl.pallas_call(
        _final_kernel,
        grid=(t // tm,),
        in_specs=[pl.BlockSpec((tm, d), lambda i: (i, 0))] + _moe_specs(t, tm, d)
        + [pl.BlockSpec((1, d), lambda i: (0, 0))],
        out_specs=pl.BlockSpec((tm, d), lambda i: (i, 0)),
        out_shape=jax.ShapeDtypeStruct((t, d), F32),
        compiler_params=_cparams(("parallel",)),
        name="final_norm",
    )(h, y, y, rw, g)


def _prep_inproj(w_in, b_forget):
    d = w_in.shape[0]
    o = np.cumsum([0, 3 * ATT_W, 3 * ATT_W, N_HEADS, 2 * CONV_CH, MLA_Q_LORA, MLA_KV_LORA, MLA_ROPE]).tolist()
    parts = [w_in[:, o[0]:o[1]], w_in[:, o[1]:o[2]], w_in[:, o[3]:o[4]],
             w_in[:, o[4]:o[5]], w_in[:, o[5]:o[6]], w_in[:, o[6]:o[7]], w_in[:, o[2]:o[3]],
             jnp.zeros((d, C_END - C_SMALL - MLA_ROPE - N_HEADS), F32)]
    w = jnp.concatenate(parts, axis=1).astype(BF16)
    bf = jnp.zeros((1, LANES), F32).at[0, SM_CUM:SM_CUM + N_HEADS].set(b_forget)
    return w, bf


def _rope_placement():
    hp, half = MLA_HEAD_PAD, MLA_ROPE // 2
    pk = np.zeros((LANES, hp * N_HEADS), np.float32)
    pkr = np.zeros((LANES, hp * N_HEADS), np.float32)
    for h in range(N_HEADS):
        for i in range(MLA_ROPE):
            pk[SM_ROPE + i, h * hp + MLA_NOPE + i] = 1.0
        for i in range(half):
            pkr[SM_ROPE + half + i, h * hp + MLA_NOPE + i] = -1.0
            pkr[SM_ROPE + i, h * hp + MLA_NOPE + half + i] = 1.0
    return jnp.asarray(pk, BF16), jnp.asarray(pkr, BF16)


def _prep_mla(wuq, wukv):
    hp, half = MLA_HEAD_PAD, MLA_ROPE // 2
    qd = MLA_NOPE + MLA_ROPE
    zq = jnp.zeros((MLA_Q_LORA, hp - qd), F32)
    wq, wqr, wk, wv = [], [], [], []
    for h in range(N_HEADS):
        nope = wuq[:, h * qd:h * qd + MLA_NOPE]
        pe = wuq[:, h * qd + MLA_NOPE:(h + 1) * qd]
        wq += [nope, pe, zq]
        wqr += [jnp.zeros_like(nope), -pe[:, half:], pe[:, :half], zq]
        kv = wukv[:, h * (MLA_NOPE + MLA_V):(h + 1) * (MLA_NOPE + MLA_V)]
        wk += [kv[:, :MLA_NOPE], jnp.zeros((MLA_KV_LORA, hp - MLA_NOPE), F32)]
        wv += [kv[:, MLA_NOPE:]]
    cat = lambda xs: jnp.concatenate(xs, axis=1).astype(BF16)
    return cat(wq), cat(wqr), cat(wk), cat(wv)


def _rope_tables(seq):
    inv_freq = ROPE_THETA ** (-jnp.arange(0, MLA_ROPE, 2, dtype=F32) / MLA_ROPE)
    ang = jnp.arange(seq, dtype=F32)[:, None] * inv_freq[None, :]
    cos, sin = jnp.cos(ang), jnp.sin(ang)
    one = jnp.ones((seq, MLA_NOPE), F32)
    zero = jnp.zeros((seq, MLA_NOPE), F32)
    pad = jnp.zeros((seq, MLA_HEAD_PAD - MLA_NOPE - MLA_ROPE), F32)
    cos_h = jnp.concatenate([one, cos, cos, pad], axis=1)
    sin_h = jnp.concatenate([zero, sin, sin, pad], axis=1)
    return jnp.tile(cos_h, (1, N_HEADS)), jnp.tile(sin_h, (1, N_HEADS))


def kernel(x, norm1_g, w_in, b_forget, conv_w, conv_b, conv_ln_g, conv_ln_b, mla_gq, mla_wuq, mla_gkv,
           mla_wukv, w_gate, b_gate, w_branch, w_out, norm2_g, w_router_group, b_router_group,
           w_router_expert, b_router_expert, w_exp_gate, w_exp_up, w_exp_down, final_g):
    batch, seq, d = x.shape
    t = batch * seq
    depth = w_in.shape[0]
    assert seq % MOBA_BLOCK == 0 and t % TOKEN_TILE == 0
    cos_t, sin_t = _rope_tables(seq)
    pk, pkr = _rope_placement()
    n_slots = 2 * t
    p_rows = n_slots + N_EXPERTS * FFN_ROWS
    n_blk = p_rows // FFN_ROWS

    h = x.reshape(t, d)
    moe = None
    for l in range(depth):
        w_l, bf_l = _prep_inproj(w_in[l], b_forget[l])
        outs = _inproj(h, norm1_g[l][None], w_l, bf_l, seq=seq, moe=moe)
        if moe is not None:
            h, *outs = outs
        moba, moba_vt, fox, fox_vt, conv_in, cq, ckv, small, cumt = outs
        y_a = _attention("moba", moba, 0, moba, 1, moba_vt, batch=batch, seq=seq)
        y_b = _attention("fox", fox, 0, fox, 1, fox_vt, batch=batch, seq=seq, extra=(small, cumt))
        conv_wp = jnp.zeros((32, CONV_CH), F32).at[:CONV_K].set(conv_w[l])
        conv_p = jnp.zeros((SUBLANES, CONV_CH), F32).at[0].set(conv_b[l]).at[1].set(conv_ln_g[l]).at[2].set(
            conv_ln_b[l])
        y_c = _conv_module(conv_in, conv_wp, conv_p, batch=batch, seq=seq)
        wq, wqr, wk, wv = _prep_mla(mla_wuq[l], mla_wukv[l])
        qd, kd, vd = _mla_prep(cq, ckv, small, cos_t, sin_t, mla_gq[l][None], mla_gkv[l][None],
                               wq, wqr, wk, wv, pk, pkr, seq=seq)
        y_d = _attention("mla", qd, 0, kd, 0, vd, batch=batch, seq=seq)
        h = _merge(h, norm1_g[l][None], (y_a, y_b, y_c, y_d), w_gate[l].astype(BF16), b_gate[l][:, None, :],
                   w_branch[l].astype(BF16), w_out[l].astype(BF16))

        wr = jnp.zeros((d, LANES), F32).at[:, :N_EXPERTS].set(w_router_expert[l]).at[
            :, N_EXPERTS:N_EXPERTS + N_GROUPS].set(w_router_group[l])
        br = jnp.zeros((1, LANES), F32).at[0, :N_EXPERTS].set(b_router_expert[l]).at[
            0, N_EXPERTS:N_EXPERTS + N_GROUPS].set(b_router_group[l])
        hn, ri, rw, cnt = _route(h, norm2_g[l][None], wr, br)
        counts = cnt[0, :N_EXPERTS].astype(I32)
        starts = jnp.cumsum(counts) - counts
        pcounts = ((counts + FFN_ROWS - 1) // FFN_ROWS) * FFN_ROWS
        pends = jnp.cumsum(pcounts)
        pstarts = pends - pcounts
        order = jnp.argsort(ri[:, 0:2].T.reshape(-1)).astype(I32)
        blk_start = jnp.arange(n_blk, dtype=I32) * FFN_ROWS
        blk_e = jnp.minimum(jnp.sum(blk_start[:, None] >= pends[None, :], axis=1), N_EXPERTS - 1).astype(I32)
        pos = jnp.arange(p_rows, dtype=I32)
        e_p = jnp.repeat(blk_e, FFN_ROWS)
        off = pos - pstarts[e_p]
        real = off < counts[e_p]
        before = starts[e_p] + jnp.minimum(off, counts[e_p])
        slot = order[jnp.minimum(before, n_slots - 1)]
        src_tok = jnp.where(real, slot % t, 0)
        dst_row = jnp.where(real, slot, n_slots + pos - before)
        w13 = jnp.concatenate([w_exp_gate[l], w_exp_up[l]], axis=-1).astype(BF16)
        y = _expert_ffn(hn, src_tok.reshape(n_blk, 1, FFN_ROWS), dst_row.reshape(n_blk, 1, FFN_ROWS), blk_e,
                        w13, w_exp_down[l].astype(BF16), y_rows=p_rows)
        moe = (y, rw)
    return _final_norm(h, *moe, final_g[None]).reshape(batch, seq, d)
```

```python
import functools

import numpy as np
import jax
import jax.numpy as jnp
from jax import lax
from jax.experimental import pallas as pl
from jax.experimental.pallas import tpu as pltpu

F32 = jnp.float32
BF16 = jnp.bfloat16
I32 = jnp.int32

HEAD_DIM = 64
N_HEADS = 4
ATT_W = N_HEADS * HEAD_DIM
MOBA_BLOCK = 256
MOBA_TOPK = 3
CONV_CH = 256
CONV_K = 31
MLA_Q_LORA = 256
MLA_KV_LORA = 128
MLA_NOPE = 64
MLA_ROPE = 32
MLA_V = 64
MLA_HEAD_PAD = 128
ROPE_THETA = 10000.0
BRANCH_W = 256
N_BRANCH = 4
N_GROUPS = 4
EXPERTS_PER_GROUP = 8
N_EXPERTS = N_GROUPS * EXPERTS_PER_GROUP
EXPERT_FF = 256
NORM_EPS = 1e-6
NEG_INF = -1e30
LOG2E = 1.4426950408889634

LANES = 128
SUBLANES = 8
VMEM_LIMIT = 56 * 1024 * 1024

TOKEN_TILE = 512
ATT_TK = 256
V_ROWS = 80
ATT_UNROLL = 4
CONV_CHUNK = 128
FFN_ROWS = 256

C_MOBA, C_FOX, C_CONV, C_CQ, C_CKV, C_SMALL, C_END = 0, 768, 1536, 2048, 2304, 2432, 2560
SM_ROPE = 0
SM_CUM = 32


def _cparams(sem):
    return pltpu.CompilerParams(dimension_semantics=sem, vmem_limit_bytes=VMEM_LIMIT)


def _rms(x, g):
    return x * lax.rsqrt(jnp.mean(x * x, axis=-1, keepdims=True) + NORM_EPS) * g


def _sigmoid(x):
    return 1.0 / (1.0 + jnp.exp(-x))


def _dot(a, b):
    return jnp.dot(a, b, preferred_element_type=F32)


def _values_t(v):
    vt = v.T
    ones = jnp.ones((V_ROWS - HEAD_DIM, vt.shape[1]), F32)
    parts = [x for h in range(N_HEADS) for x in (vt[h * HEAD_DIM:(h + 1) * HEAD_DIM, :], ones)]
    return jnp.concatenate(parts, axis=0).astype(BF16)


def _moe_combine(h_ref, moe_refs):
    ya_ref, yb_ref, rw_ref = moe_refs
    rw = rw_ref[...]
    return h_ref[...] + rw[:, 0:1] * _load_row_tiles(ya_ref) + rw[:, 1:2] * _load_row_tiles(yb_ref)


def _inproj_kernel(*refs, tiles_per_seq, combine):
    if combine:
        h_ref, *moe_refs, g_ref, w_ref, bf_ref, hout_ref = refs[:8]
        refs = refs[8:]
        h = _moe_combine(h_ref, moe_refs)
        hout_ref[...] = h
    else:
        h_ref, g_ref, w_ref, bf_ref = refs[:4]
        refs = refs[4:]
        h = h_ref[...]
    moba_ref, mvt_ref, fox_ref, fvt_ref, conv_ref, cq_ref, ckv_ref, small_ref, cumt_ref, carry_ref = refs
    i = pl.program_id(0)
    tm = h_ref.shape[0]
    xn = _rms(h, g_ref[...]).astype(BF16)

    def proj(a, b):
        return _dot(xn, w_ref[:, a:b])

    for qk_ref, vt_ref, c0 in ((moba_ref, mvt_ref, C_MOBA), (fox_ref, fvt_ref, C_FOX)):
        qk_ref[:, :ATT_W] = (proj(c0, c0 + ATT_W) * (HEAD_DIM ** -0.5 * LOG2E)).astype(BF16)
        qk_ref[:, ATT_W:] = proj(c0 + ATT_W, c0 + 2 * ATT_W).astype(BF16)
        vt_ref[0] = _values_t(proj(c0 + 2 * ATT_W, c0 + 3 * ATT_W))
    conv_ref[...] = proj(C_CONV, C_CQ).astype(BF16)
    cq_ref[...] = proj(C_CQ, C_CKV).astype(BF16)
    ckv_ref[...] = proj(C_CKV, C_SMALL).astype(BF16)
    sm = proj(C_SMALL, C_END)

    f = sm + bf_ref[...]
    c = jnp.minimum(f, 0.0) - jnp.log(1.0 + jnp.exp(-jnp.abs(f)))
    row = lax.broadcasted_iota(I32, c.shape, 0)
    sh = 1
    while sh < tm:
        c = c + jnp.where(row >= sh, pltpu.roll(c, sh, 0), 0.0)
        sh *= 2

    @pl.when(i % tiles_per_seq == 0)
    def _():
        carry_ref[...] = jnp.zeros_like(carry_ref)

    c = c + carry_ref[...]
    carry_ref[...] = c[tm - 1:tm, :]
    lane = lax.broadcasted_iota(I32, c.shape, 1)
    small_ref[...] = jnp.where((lane >= SM_CUM) & (lane < SM_CUM + N_HEADS), c, sm)
    cumt_ref[0] = c.T[SM_CUM:SM_CUM + SUBLANES, :]


def _moe_specs(t, tm, d):
    assert d == ROW_TILE * LANES
    return [pl.BlockSpec((tm * ROW_TILE, LANES), lambda i: (i, 0)),
            pl.BlockSpec((tm * ROW_TILE, LANES), lambda i: (i + t // tm, 0)),
            pl.BlockSpec((tm, LANES), lambda i: (i, 0))]


def _inproj(h, g, w, bf, *, seq, moe=None):
    t, d = h.shape
    tm = min(TOKEN_TILE, seq)
    tiles_per_seq = seq // tm
    nb = t // seq
    row = lambda i: (i, 0)
    fixed = lambda i: (0, 0)
    seq_t = lambda i: (i // tiles_per_seq, 0, i % tiles_per_seq)
    tok = lambda wdt, dt: (pl.BlockSpec((tm, wdt), row), jax.ShapeDtypeStruct((t, wdt), dt))
    chan = lambda c, dt: (pl.BlockSpec((1, c, tm), seq_t), jax.ShapeDtypeStruct((nb, c, seq), dt))
    outs = [tok(2 * ATT_W, BF16), chan(N_HEADS * V_ROWS, BF16), tok(2 * ATT_W, BF16), chan(N_HEADS * V_ROWS, BF16),
            tok(C_CQ - C_CONV, BF16), tok(C_CKV - C_CQ, BF16), tok(C_SMALL - C_CKV, BF16),
            tok(C_END - C_SMALL, F32), chan(SUBLANES, F32)]
    if moe is not None:
        outs = [tok(d, F32)] + outs
    return pl.pallas_call(
        functools.partial(_inproj_kernel, tiles_per_seq=tiles_per_seq, combine=moe is not None),
        grid=(t // tm,),
        in_specs=[pl.BlockSpec((tm, d), row)] + (_moe_specs(t, tm, d) if moe is not None else [])
        + [pl.BlockSpec((1, d), fixed), pl.BlockSpec((d, C_END), fixed), pl.BlockSpec((1, LANES), fixed)],
        out_specs=[o[0] for o in outs],
        out_shape=[o[1] for o in outs],
        scratch_shapes=[pltpu.VMEM((1, LANES), F32)],
        compiler_params=_cparams(("arbitrary",)),
        name="inproj",
    )(h, *((moe[0], moe[0], moe[1]) if moe is not None else ()), g, w, bf)


def _attn_kernel(*refs, mode, tq, tk, nblk, n_sel):
    qs_ref, ua_ref, ub_ref, mba_ref, mbb_ref, m_ref, acc_ref = refs[-7:]
    refs = refs[:-7]
    if mode == "fox":
        q_ref, k_ref, vt_ref, small_ref, cumt_ref, o_ref, csb_ref = refs
    elif mode == "moba":
        q_ref, k_ref, vt_ref, o_ref, kmean_ref, selb_ref = refs
    else:
        q_ref, k_ref, vt_ref, o_ref = refs
    i = pl.program_id(1)
    nh = N_HEADS
    rows = nh * tq
    seq = k_ref.shape[1]
    n_full = (i * tq) // tk
    q0 = i * tq

    q = q_ref[0]
    if mode == "mla":
        for h in range(nh):
            qs_ref[:, h * tq:(h + 1) * tq] = q[:, h * MLA_HEAD_PAD:(h + 1) * MLA_HEAD_PAD].astype(F32).T.astype(BF16)
    else:
        qt = q.astype(F32).T.astype(BF16)
        sub_h = lax.shift_right_logical(lax.broadcasted_iota(I32, (ATT_W, tq), 0), 6)
        for h in range(nh):
            qs_ref[:, h * tq:(h + 1) * tq] = jnp.where(sub_h == h, qt, jnp.zeros_like(qt))

    r_in = lax.broadcasted_iota(I32, (tk, rows), 1) & (tq - 1)
    c_in = lax.broadcasted_iota(I32, (tk, rows), 0)

    def per_head_rows(x):
        return jnp.concatenate(
            [jnp.broadcast_to(x[:, h * tq:(h + 1) * tq], (V_ROWS, tq)) for h in range(nh)], axis=0)

    if mode == "fox":
        @pl.when(i == 0)
        def _():
            def fill(c, _):
                r0 = pl.multiple_of(c * tk, tk)
                cs = small_ref[0, pl.ds(r0, tk), :]
                for h in range(nh):
                    csb_ref[pl.ds(r0, tk), h * LANES:(h + 1) * LANES] = jnp.broadcast_to(
                        -LOG2E * cs[:, SM_CUM + h:SM_CUM + h + 1], (tk, LANES))
                return 0
            lax.fori_loop(0, seq // tk, fill, 0)

        ct = cumt_ref[0] * LOG2E
        c_t = jnp.concatenate([ct[h:h + 1, :] for h in range(nh)], axis=1)
    if mode == "moba":
        slopes = [LOG2E * 2.0 ** (-8.0 * (h + 1) / nh) for h in range(nh)]
        slope_row = jnp.concatenate([jnp.full((1, tq), s, F32) for s in slopes], axis=1)
        alibi_key = slope_row * c_in.astype(F32)
        r_row = (lax.broadcasted_iota(I32, (1, rows), 1) & (tq - 1)).astype(F32)
        nbp = kmean_ref.shape[0]

        @pl.when(i == 0)
        def _():
            kmean_ref[...] = jnp.zeros_like(kmean_ref)
            for n in range(nblk):
                kb = k_ref[0, n * MOBA_BLOCK:(n + 1) * MOBA_BLOCK, :].astype(F32)
                kmean_ref[n:n + 1, :] = jnp.mean(kb, axis=0, keepdims=True)

        km = kmean_ref[...]
        km_hi = km.astype(BF16)
        km_lo = (km - km_hi.astype(F32)).astype(BF16)
        cur = (i * tq) // MOBA_BLOCK
        qs = qs_ref[...]
        gate = _dot(km_hi, qs) + _dot(km_lo, qs)
        n_iota = lax.broadcasted_iota(I32, gate.shape, 0)
        gate = jnp.where(n_iota < cur, gate, NEG_INF)
        rank = jnp.zeros(gate.shape, I32)
        for n2 in range(nblk):
            gb = gate[n2:n2 + 1, :]
            ahead = jnp.where(gb > gate, 1, jnp.where((gb == gate) & (n_iota > n2), 1, 0))
            rank = rank + ahead
        selb_ref[...] = jnp.where((rank < n_sel) & (n_iota < cur), 0.0, NEG_INF)

    def produce(u_ref, mb_ref, j, diagonal=False):
        k0 = pl.multiple_of(j * tk, tk)
        kj = k_ref[0, pl.ds(k0, tk), :]
        if mode == "mla":
            u = jnp.concatenate(
                [_dot(kj[:, h * MLA_HEAD_PAD:(h + 1) * MLA_HEAD_PAD], qs_ref[:, h * tq:(h + 1) * tq])
                 for h in range(nh)], axis=1)
        else:
            u = _dot(kj, qs_ref[...])
        if mode == "fox":
            cs = csb_ref[pl.ds(k0, tk), :]
            u = u + jnp.concatenate([cs[:, h * LANES:(h + 1) * LANES] for h in range(nh)
                                     for _ in range(tq // LANES)], axis=1)
        if mode == "moba":
            u = u + alibi_key
        if diagonal:
            u = jnp.where(c_in + j * tk <= r_in + q0, u, NEG_INF)
        u_ref[...] = u
        mb_ref[...] = jnp.max(u, axis=0, keepdims=True)

    def consume(u_ref, mb_ref, j, diagonal=False):
        vtj = vt_ref[0, :, pl.ds(pl.multiple_of(j * tk, tk), tk)]
        m = m_ref[...]
        m_blk = mb_ref[...]
        if mode == "fox":
            row = c_t
        if mode == "moba":
            row = slope_row * ((j * tk - q0).astype(F32) - r_row)
            if not diagonal:
                row = row + selb_ref[pl.ds(j, 1), :]
        if mode != "mla":
            m_blk = m_blk + row
        m_new = jnp.maximum(m, m_blk)
        alpha = jnp.exp2(m - m_new)
        shift = m_new - row if mode != "mla" else m_new
        os_ = []
        for h in range(nh):
            cols = slice(h * tq, (h + 1) * tq)
            p = jnp.exp2(u_ref[:, cols] - shift[:, cols])
            os_.append(_dot(vtj[h * V_ROWS:(h + 1) * V_ROWS, :], p.astype(BF16)))
        m_ref[...] = m_new
        acc_ref[...] = per_head_rows(alpha) * acc_ref[...] + jnp.concatenate(os_, axis=0)

    m_ref[...] = jnp.full(m_ref.shape, NEG_INF, F32)
    acc_ref[...] = jnp.zeros(acc_ref.shape, F32)
    produce(ub_ref, mbb_ref, n_full, True)
    produce(ua_ref, mba_ref, 0)
    consume(ub_ref, mbb_ref, n_full, True)

    bufs = ((ua_ref, mba_ref), (ub_ref, mbb_ref))

    def trip(t, _):
        for k in range(ATT_UNROLL):
            produce(*bufs[(k + 1) % 2], ATT_UNROLL * t + k + 1)
            consume(*bufs[k % 2], ATT_UNROLL * t + k)
        return 0

    lax.fori_loop(0, n_full // ATT_UNROLL, trip, 0)
    base = (n_full // ATT_UNROLL) * ATT_UNROLL
    for k in range(ATT_UNROLL - 1):
        @pl.when(n_full - base > k)
        def _():
            if k + 1 < ATT_UNROLL - 1:
                produce(*bufs[(k + 1) % 2], jnp.minimum(base + k + 1, n_full))
            consume(*bufs[k % 2], base + k)

    acc = acc_ref[...]
    out_t = jnp.concatenate(
        [acc[h * V_ROWS:h * V_ROWS + HEAD_DIM, :] * (1.0 / acc[h * V_ROWS + HEAD_DIM:h * V_ROWS + HEAD_DIM + 1, :])
         for h in range(nh)], axis=0)
    o_ref[0] = out_t.T.astype(BF16)


def _attention(mode, q_arr, q_blk, k_arr, k_blk, vt_arr, *, batch, seq, extra=()):
    tq = tk = min(ATT_TK, seq)
    assert tq % LANES == 0 and seq % tk == 0
    dq = MLA_HEAD_PAD * N_HEADS if mode == "mla" else ATT_W
    nblk = seq // MOBA_BLOCK
    rows = N_HEADS * tq
    r3 = lambda a: a.reshape(batch, seq, a.shape[-1])
    in_specs = [pl.BlockSpec((1, tq, dq), lambda b, i: (b, i, q_blk)),
                pl.BlockSpec((1, seq, dq), lambda b, i: (b, 0, k_blk)),
                pl.BlockSpec((1, N_HEADS * V_ROWS, seq), lambda b, i: (b, 0, 0))]
    args = [r3(q_arr), r3(k_arr), vt_arr]
    scratch = []
    if mode == "fox":
        small, cumt = extra
        in_specs += [pl.BlockSpec((1, seq, LANES), lambda b, i: (b, 0, 0)),
                     pl.BlockSpec((1, SUBLANES, tq), lambda b, i: (b, 0, i))]
        args += [r3(small), cumt]
        scratch = [pltpu.VMEM((seq, N_HEADS * LANES), F32)]
    if mode == "moba":
        nbp = -(-nblk // SUBLANES) * SUBLANES
        scratch = [pltpu.VMEM((nbp, ATT_W), F32), pltpu.VMEM((nbp, rows), F32)]
    scratch += [pltpu.VMEM((dq // N_HEADS if mode == "mla" else ATT_W, rows), BF16),
                pltpu.VMEM((tk, rows), F32), pltpu.VMEM((tk, rows), F32),
                pltpu.VMEM((1, rows), F32), pltpu.VMEM((1, rows), F32),
                pltpu.VMEM((1, rows), F32),
                pltpu.VMEM((N_HEADS * V_ROWS, tq), F32)]
    out = pl.pallas_call(
        functools.partial(_attn_kernel, mode=mode, tq=tq, tk=tk, nblk=nblk, n_sel=min(MOBA_TOPK, nblk - 1)),
        grid=(batch, seq // tq),
        in_specs=in_specs,
        out_specs=pl.BlockSpec((1, tq, ATT_W), lambda b, i: (b, i, 0)),
        out_shape=jax.ShapeDtypeStruct((batch, seq, ATT_W), BF16),
        scratch_shapes=scratch,
        compiler_params=_cparams(("parallel", "arbitrary")),
        name="attn_" + mode,
    )(*args)
    return out.reshape(batch * seq, ATT_W)


def _conv_kernel(x_ref, w_ref, p_ref, o_ref, u_ref, *, chunk):
    seq = x_ref.shape[1]
    halo = 32
    u_ref[0:halo, :] = jnp.zeros((halo, CONV_CH), F32)

    def glu(c, _):
        r0 = pl.multiple_of(c * chunk, chunk)
        x = x_ref[0, pl.ds(r0, chunk), :].astype(F32)
        u_ref[pl.ds(halo + r0, chunk), :] = x[:, :CONV_CH] * _sigmoid(x[:, CONV_CH:])
        return 0

    lax.fori_loop(0, seq // chunk, glu, 0)
    bias, ln_g, ln_b = p_ref[0:1, :], p_ref[1:2, :], p_ref[2:3, :]

    def conv(c, _):
        r0 = pl.multiple_of(c * chunk, chunk)
        n = chunk + halo
        win = u_ref[pl.ds(r0, n), :]
        acc = jnp.zeros((chunk, CONV_CH), F32)
        for b in range(SUBLANES):
            shifted = win if b == 0 else pltpu.roll(win, n - b, 0)
            for a in range(halo // SUBLANES + 1):
                k = SUBLANES * a + b - (halo - (CONV_K - 1))
                if 0 <= k < CONV_K:
                    acc = acc + w_ref[k:k + 1, :] * shifted[SUBLANES * a:SUBLANES * a + chunk, :]
        acc = acc + bias
        mu = jnp.mean(acc, axis=-1, keepdims=True)
        xc = acc - mu
        var = jnp.mean(xc * xc, axis=-1, keepdims=True)
        y = xc * lax.rsqrt(var + NORM_EPS) * ln_g + ln_b
        o_ref[0, pl.ds(r0, chunk), :] = (y * _sigmoid(y)).astype(BF16)
        return 0

    lax.fori_loop(0, seq // chunk, conv, 0)


def _conv_module(conv_in, w, p, *, batch, seq):
    chunk = min(CONV_CHUNK, seq)
    out = pl.pallas_call(
        functools.partial(_conv_kernel, chunk=chunk),
        grid=(batch,),
        in_specs=[pl.BlockSpec((1, seq, 2 * CONV_CH), lambda b: (b, 0, 0)),
                  pl.BlockSpec((32, CONV_CH), lambda b: (0, 0)),
                  pl.BlockSpec((SUBLANES, CONV_CH), lambda b: (0, 0))],
        out_specs=pl.BlockSpec((1, seq, CONV_CH), lambda b: (b, 0, 0)),
        out_shape=jax.ShapeDtypeStruct((batch, seq, CONV_CH), BF16),
        scratch_shapes=[pltpu.VMEM((seq + 32, CONV_CH), F32)],
        compiler_params=_cparams(("parallel",)),
        name="conv",
    )(conv_in.reshape(batch, seq, 2 * CONV_CH), w, p)
    return out.reshape(batch * seq, CONV_CH)


def _mla_prep_kernel(cq_ref, ckv_ref, small_ref, cos_ref, sin_ref, gq_ref, gkv_ref, wq_ref, wqr_ref,
                     wk_ref, wv_ref, pk_ref, pkr_ref, q_ref, k_ref, v_ref, *, scale):
    cos, sin = cos_ref[...], sin_ref[...]
    cq = _rms(cq_ref[...].astype(F32), gq_ref[...]).astype(BF16)
    q = _dot(cq, wq_ref[...]) * cos + _dot(cq, wqr_ref[...]) * sin
    q_ref[...] = (q * scale).astype(BF16)
    ckv = _rms(ckv_ref[...].astype(F32), gkv_ref[...]).astype(BF16)
    kr = small_ref[...].astype(BF16)
    k = _dot(ckv, wk_ref[...]) + _dot(kr, pk_ref[...]) * cos + _dot(kr, pkr_ref[...]) * sin
    k_ref[...] = k.astype(BF16)
    v_ref[0] = _values_t(_dot(ckv, wv_ref[...]))


def _mla_prep(cq, ckv, small, cos_t, sin_t, gq, gkv, wq, wqr, wk, wv, pk, pkr, *, seq):
    t = cq.shape[0]
    tm = min(TOKEN_TILE, seq)
    tiles_per_seq = seq // tm
    wfull = MLA_HEAD_PAD * N_HEADS
    row = lambda i: (i, 0)
    pos = lambda i: (i % tiles_per_seq, 0)
    fixed = lambda i: (0, 0)
    return pl.pallas_call(
        functools.partial(_mla_prep_kernel, scale=float((MLA_NOPE + MLA_ROPE) ** -0.5 * LOG2E)),
        grid=(t // tm,),
        in_specs=[pl.BlockSpec((tm, MLA_Q_LORA), row), pl.BlockSpec((tm, MLA_KV_LORA), row),
                  pl.BlockSpec((tm, LANES), row),
                  pl.BlockSpec((tm, wfull), pos), pl.BlockSpec((tm, wfull), pos),
                  pl.BlockSpec((1, MLA_Q_LORA), fixed), pl.BlockSpec((1, MLA_KV_LORA), fixed),
                  pl.BlockSpec((MLA_Q_LORA, wfull), fixed), pl.BlockSpec((MLA_Q_LORA, wfull), fixed),
                  pl.BlockSpec((MLA_KV_LORA, wfull), fixed), pl.BlockSpec((MLA_KV_LORA, ATT_W), fixed),
                  pl.BlockSpec((LANES, wfull), fixed), pl.BlockSpec((LANES, wfull), fixed)],
        out_specs=[pl.BlockSpec((tm, wfull), row), pl.BlockSpec((tm, wfull), row),
                   pl.BlockSpec((1, N_HEADS * V_ROWS, tm), lambda i: (i // tiles_per_seq, 0, i % tiles_per_seq))],
        out_shape=[jax.ShapeDtypeStruct((t, wfull), BF16), jax.ShapeDtypeStruct((t, wfull), BF16),
                   jax.ShapeDtypeStruct((t // seq, N_HEADS * V_ROWS, seq), BF16)],
        compiler_params=_cparams(("parallel",)),
        name="mla_prep",
    )(cq, ckv, small, cos_t, sin_t, gq, gkv, wq, wqr, wk, wv, pk, pkr)


def _merge_kernel(h_ref, g_ref, ya_ref, yb_ref, yc_ref, yd_ref, wg_ref, bg_ref, wb_ref, wo_ref, o_ref):
    h = h_ref[...]
    xn = _rms(h, g_ref[...]).astype(BF16)
    merged = jnp.zeros(h.shape, F32)
    for i, y_ref in enumerate((ya_ref, yb_ref, yc_ref, yd_ref)):
        gate = _sigmoid(_dot(xn, wg_ref[i]) + bg_ref[i])
        merged = merged + gate * _dot(y_ref[...], wb_ref[i])
    o_ref[...] = h + _dot(merged.astype(BF16), wo_ref[...])


def _merge(h, g, ys, wg, bg, wb, wo):
    t, d = h.shape
    tm = min(TOKEN_TILE, t)
    row = lambda i: (i, 0)
    fixed2 = lambda i: (0, 0)
    fixed3 = lambda i: (0, 0, 0)
    return pl.pallas_call(
        _merge_kernel,
        grid=(t // tm,),
        in_specs=[pl.BlockSpec((tm, d), row), pl.BlockSpec((1, d), fixed2)]
        + [pl.BlockSpec((tm, BRANCH_W), row)] * N_BRANCH
        + [pl.BlockSpec((N_BRANCH, d, d), fixed3), pl.BlockSpec((N_BRANCH, 1, d), fixed3),
           pl.BlockSpec((N_BRANCH, BRANCH_W, d), fixed3), pl.BlockSpec((d, d), fixed2)],
        out_specs=pl.BlockSpec((tm, d), row),
        out_shape=jax.ShapeDtypeStruct((t, d), F32),
        compiler_params=_cparams(("parallel",)),
        name="merge",
    )(h, g, *ys, wg, bg, wb, wo)


ROW_TILE = 8


def _store_row_tiles(ref, x):
    m = x.shape[0]
    for j in range(ROW_TILE):
        ref[pl.ds(j, m, stride=ROW_TILE), :] = x[:, j * LANES:(j + 1) * LANES]


def _load_row_tiles(ref):
    m = ref.shape[0] // ROW_TILE
    return jnp.concatenate([ref[pl.ds(j, m, stride=ROW_TILE), :] for j in range(ROW_TILE)], axis=1)


def _route_kernel(h_ref, g_ref, wr_ref, br_ref, hn_ref, ri_ref, rw_ref, cnt_ref, carry_ref):
    i = pl.program_id(0)
    hn = _rms(h_ref[...], g_ref[...])
    _store_row_tiles(hn_ref, hn)
    lg = jnp.dot(hn, wr_ref[...], precision=lax.Precision.HIGHEST, preferred_element_type=F32) + br_ref[...]
    lane = lax.broadcasted_iota(I32, lg.shape, 1)
    big = jnp.int32(4 * LANES)

    def top1(vals):
        mx = jnp.max(vals, axis=1, keepdims=True)
        idx = jnp.min(jnp.where(vals == mx, lane, big), axis=1, keepdims=True)
        return mx, idx

    is_g = (lane >= N_EXPERTS) & (lane < N_EXPERTS + N_GROUPS)
    gmax, gidx = top1(jnp.where(is_g, lg, NEG_INF))
    p_sel = 1.0 / jnp.sum(jnp.where(is_g, jnp.exp(lg - gmax), 0.0), axis=1, keepdims=True)
    lo = (gidx - N_EXPERTS) * EXPERTS_PER_GROUP
    el = jnp.where((lane >= lo) & (lane < lo + EXPERTS_PER_GROUP), lg, NEG_INF)
    l1, i1 = top1(el)
    l2, i2 = top1(jnp.where(lane == i1, NEG_INF, el))
    e = jnp.exp(l2 - l1)
    w1 = p_sel / (1.0 + e)
    w2 = p_sel * e / (1.0 + e)

    @pl.when(i == 0)
    def _():
        carry_ref[...] = jnp.zeros_like(carry_ref)

    onehot = jnp.where((lane == i1) | (lane == i2), 1.0, 0.0)
    carry_ref[...] = carry_ref[...] + jnp.sum(onehot, axis=0, keepdims=True)
    cnt_ref[...] = carry_ref[...]
    ri_ref[...] = jnp.where(lane == 0, i1, jnp.where(lane == 1, i2, 0))
    rw_ref[...] = jnp.where(lane == 0, w1, jnp.where(lane == 1, w2, 0.0))


def _route(h, g, wr, br):
    t, d = h.shape
    tm = min(TOKEN_TILE, t)
    row = lambda i: (i, 0)
    fixed = lambda i: (0, 0)
    return pl.pallas_call(
        _route_kernel,
        grid=(t // tm,),
        in_specs=[pl.BlockSpec((tm, d), row), pl.BlockSpec((1, d), fixed),
                  pl.BlockSpec((d, LANES), fixed), pl.BlockSpec((1, LANES), fixed)],
        out_specs=[pl.BlockSpec((tm * ROW_TILE, LANES), row), pl.BlockSpec((tm, LANES), row),
                   pl.BlockSpec((tm, LANES), row), pl.BlockSpec((1, LANES), fixed)],
        out_shape=[jax.ShapeDtypeStruct((t * ROW_TILE, LANES), F32), jax.ShapeDtypeStruct((t, LANES), I32),
                   jax.ShapeDtypeStruct((t, LANES), F32), jax.ShapeDtypeStruct((1, LANES), F32)],
        scratch_shapes=[pltpu.VMEM((1, LANES), F32)],
        compiler_params=_cparams(("arbitrary",)),
        name="route",
    )(h, g, wr, br)


def _ffn_kernel(blk_e_ref, src0_ref, src1_ref, src2_ref, dst0_ref, dst1_ref, hn_hbm, w13_ref, w2_ref, y_hbm,
                src_s, dst_s, xa, xb, oa, ob, sem_idx, sem_in, sem_out):
    i = pl.program_id(0)
    n = pl.num_programs(0)
    xbufs, obufs = (xa, xb), (oa, ob)
    rows = xa.shape[0] // ROW_TILE

    def src_copy(vmem_ref, p):
        return pltpu.make_async_copy(vmem_ref.at[0, 0], src_s.at[p], sem_idx.at[0])

    def dst_copy(vmem_ref, p):
        return pltpu.make_async_copy(vmem_ref.at[0, 0], dst_s.at[p], sem_idx.at[1])

    def tile(ref, r):
        return ref.at[pl.ds(pl.multiple_of(r * ROW_TILE, ROW_TILE), ROW_TILE)]

    def start_gather(s):
        for r in range(rows):
            pltpu.make_async_copy(tile(hn_hbm, src_s[s, r]), tile(xbufs[s], r), sem_in.at[s]).start(priority=r % 2)

    def start_scatter(s):
        for r in range(rows):
            pltpu.make_async_copy(tile(obufs[s], r), tile(y_hbm, dst_s[s, r]), sem_out.at[s]).start(priority=r % 2)

    def wait_gather(s):
        pltpu.make_async_copy(hn_hbm.at[pl.ds(0, rows * ROW_TILE)], xbufs[s], sem_in.at[s]).wait()

    def wait_scatter(s):
        pltpu.make_async_copy(obufs[s], y_hbm.at[pl.ds(0, rows * ROW_TILE)], sem_out.at[s]).wait()

    @pl.when(i == 0)
    def _():
        first = src_copy(src0_ref, 0)
        first.start()
        first.wait()
        start_gather(0)
        second, third = src_copy(src1_ref, 1), dst_copy(dst0_ref, 0)
        second.start()
        third.start()
        second.wait()
        third.wait()

    def block(s):
        @pl.when(i + 1 < n)
        def _():
            start_gather(1 - s)

        nxt_src, nxt_dst = src_copy(src2_ref, s), dst_copy(dst1_ref, 1 - s)
        nxt_src.start()
        nxt_dst.start()
        wait_gather(s)

        @pl.when(i >= 2)
        def _():
            wait_scatter(s)

        x = _load_row_tiles(xbufs[s]).astype(BF16)
        ab = _dot(x, w13_ref[0])
        a, b = ab[:, :EXPERT_FF], ab[:, EXPERT_FF:]
        act = (a * _sigmoid(a) * b).astype(BF16)
        _store_row_tiles(obufs[s], _dot(act, w2_ref[0]))
        start_scatter(s)

        nxt_src.wait()
        nxt_dst.wait()

        @pl.when(i == n - 1)
        def _():
            wait_scatter(s)
            wait_scatter(1 - s)

    for s in range(2):
        pl.when(i % 2 == s)(functools.partial(block, s))


def _expert_ffn(hn, src_tok, dst_row, blk_e, w13, w2, *, y_rows):
    d = ROW_TILE * LANES
    n_blk = src_tok.shape[0]
    assert n_blk >= 2
    idx_blk = (1, 1, FFN_ROWS)
    tile_buf = pltpu.VMEM((FFN_ROWS * ROW_TILE, LANES), F32)
    grid_spec = pltpu.PrefetchScalarGridSpec(
        num_scalar_prefetch=1,
        grid=(n_blk,),
        in_specs=[pl.BlockSpec(idx_blk, lambda i, be: (i, 0, 0)),
                  pl.BlockSpec(idx_blk, lambda i, be: (jnp.minimum(i + 1, n_blk - 1), 0, 0)),
                  pl.BlockSpec(idx_blk, lambda i, be: (jnp.minimum(i + 2, n_blk - 1), 0, 0)),
                  pl.BlockSpec(idx_blk, lambda i, be: (i, 0, 0)),
                  pl.BlockSpec(idx_blk, lambda i, be: (jnp.minimum(i + 1, n_blk - 1), 0, 0)),
                  pl.BlockSpec(memory_space=pl.ANY),
                  pl.BlockSpec((1, d, 2 * EXPERT_FF), lambda i, be: (be[i], 0, 0)),
                  pl.BlockSpec((1, EXPERT_FF, d), lambda i, be: (be[i], 0, 0))],
        out_specs=pl.BlockSpec(memory_space=pl.ANY),
        scratch_shapes=[pltpu.SMEM((2, FFN_ROWS), I32), pltpu.SMEM((2, FFN_ROWS), I32),
                        tile_buf, tile_buf, tile_buf, tile_buf,
                        pltpu.SemaphoreType.DMA((2,)), pltpu.SemaphoreType.DMA((2,)),
                        pltpu.SemaphoreType.DMA((2,))],
    )
    return pl.pallas_call(
        _ffn_kernel,
        grid_spec=grid_spec,
        out_shape=jax.ShapeDtypeStruct((y_rows * ROW_TILE, LANES), F32),
        compiler_params=_cparams(("arbitrary",)),
        name="expert_ffn",
    )(blk_e, src_tok, src_tok, src_tok, dst_row, dst_row, hn, w13, w2)


def _final_kernel(h_ref, ya_ref, yb_ref, rw_ref, g_ref, o_ref):
    o_ref[...] = _rms(_moe_combine(h_ref, (ya_ref, yb_ref, rw_ref)), g_ref[...])


def _final_norm(h, y, rw, g):
    t, d = h.shape
    tm = min(TOKEN_TILE, t)
    return pl.pallas_call(
        _final_kernel,
        grid=(t // tm,),
        in_specs=[pl.BlockSpec((tm, d), lambda i: (i, 0))] + _moe_specs(t, tm, d)
        + [pl.BlockSpec((1, d), lambda i: (0, 0))],
        out_specs=pl.BlockSpec((tm, d), lambda i: (i, 0)),
        out_shape=jax.ShapeDtypeStruct((t, d), F32),
        compiler_params=_cparams(("parallel",)),
        name="final_norm",
    )(h, y, y, rw, g)


def _prep_inproj(w_in, b_forget):
    d = w_in.shape[0]
    o = np.cumsum([0, 3 * ATT_W, 3 * ATT_W, N_HEADS, 2 * CONV_CH, MLA_Q_LORA, MLA_KV_LORA, MLA_ROPE]).tolist()
    parts = [w_in[:, o[0]:o[1]], w_in[:, o[1]:o[2]], w_in[:, o[3]:o[4]],
             w_in[:, o[4]:o[5]], w_in[:, o[5]:o[6]], w_in[:, o[6]:o[7]], w_in[:, o[2]:o[3]],
             jnp.zeros((d, C_END - C_SMALL - MLA_ROPE - N_HEADS), F32)]
    w = jnp.concatenate(parts, axis=1).astype(BF16)
    bf = jnp.zeros((1, LANES), F32).at[0, SM_CUM:SM_CUM + N_HEADS].set(b_forget)
    return w, bf


def _rope_placement():
    hp, half = MLA_HEAD_PAD, MLA_ROPE // 2
    pk = np.zeros((LANES, hp * N_HEADS), np.float32)
    pkr = np.zeros((LANES, hp * N_HEADS), np.float32)
    for h in range(N_HEADS):
        for i in range(MLA_ROPE):
            pk[SM_ROPE + i, h * hp + MLA_NOPE + i] = 1.0
        for i in range(half):
            pkr[SM_ROPE + half + i, h * hp + MLA_NOPE + i] = -1.0
            pkr[SM_ROPE + i, h * hp + MLA_NOPE + half + i] = 1.0
    return jnp.asarray(pk, BF16), jnp.asarray(pkr, BF16)


def _prep_mla(wuq, wukv):
    hp, half = MLA_HEAD_PAD, MLA_ROPE // 2
    qd = MLA_NOPE + MLA_ROPE
    zq = jnp.zeros((MLA_Q_LORA, hp - qd), F32)
    wq, wqr, wk, wv = [], [], [], []
    for h in range(N_HEADS):
        nope = wuq[:, h * qd:h * qd + MLA_NOPE]
        pe = wuq[:, h * qd + MLA_NOPE:(h + 1) * qd]
        wq += [nope, pe, zq]
        wqr += [jnp.zeros_like(nope), -pe[:, half:], pe[:, :half], zq]
        kv = wukv[:, h * (MLA_NOPE + MLA_V):(h + 1) * (MLA_NOPE + MLA_V)]
        wk += [kv[:, :MLA_NOPE], jnp.zeros((MLA_KV_LORA, hp - MLA_NOPE), F32)]
        wv += [kv[:, MLA_NOPE:]]
    cat = lambda xs: jnp.concatenate(xs, axis=1).astype(BF16)
    return cat(wq), cat(wqr), cat(wk), cat(wv)


def _rope_tables(seq):
    inv_freq = ROPE_THETA ** (-jnp.arange(0, MLA_ROPE, 2, dtype=F32) / MLA_ROPE)
    ang = jnp.arange(seq, dtype=F32)[:, None] * inv_freq[None, :]
    cos, sin = jnp.cos(ang), jnp.sin(ang)
    one = jnp.ones((seq, MLA_NOPE), F32)
    zero = jnp.zeros((seq, MLA_NOPE), F32)
    pad = jnp.zeros((seq, MLA_HEAD_PAD - MLA_NOPE - MLA_ROPE), F32)
    cos_h = jnp.concatenate([one, cos, cos, pad], axis=1)
    sin_h = jnp.concatenate([zero, sin, sin, pad], axis=1)
    return jnp.tile(cos_h, (1, N_HEADS)), jnp.tile(sin_h, (1, N_HEADS))


def kernel(x, norm1_g, w_in, b_forget, conv_w, conv_b, conv_ln_g, conv_ln_b, mla_gq, mla_wuq, mla_gkv,
           mla_wukv, w_gate, b_gate, w_branch, w_out, norm2_g, w_router_group, b_router_group,
           w_router_expert, b_router_expert, w_exp_gate, w_exp_up, w_exp_down, final_g):
    batch, seq, d = x.shape
    t = batch * seq
    depth = w_in.shape[0]
    assert seq % MOBA_BLOCK == 0 and t % TOKEN_TILE == 0
    cos_t, sin_t = _rope_tables(seq)
    pk, pkr = _rope_placement()
    n_slots = 2 * t
    p_rows = n_slots + N_EXPERTS * FFN_ROWS
    n_blk = p_rows // FFN_ROWS

    h = x.reshape(t, d)
    moe = None
    for l in range(depth):
        w_l, bf_l = _prep_inproj(w_in[l], b_forget[l])
        outs = _inproj(h, norm1_g[l][None], w_l, bf_l, seq=seq, moe=moe)
        if moe is not None:
            h, *outs = outs
        moba, moba_vt, fox, fox_vt, conv_in, cq, ckv, small, cumt = outs
        y_a = _attention("moba", moba, 0, moba, 1, moba_vt, batch=batch, seq=seq)
        y_b = _attention("fox", fox, 0, fox, 1, fox_vt, batch=batch, seq=seq, extra=(small, cumt))
        conv_wp = jnp.zeros((32, CONV_CH), F32).at[:CONV_K].set(conv_w[l])
        conv_p = jnp.zeros((SUBLANES, CONV_CH), F32).at[0].set(conv_b[l]).at[1].set(conv_ln_g[l]).at[2].set(
            conv_ln_b[l])
        y_c = _conv_module(conv_in, conv_wp, conv_p, batch=batch, seq=seq)
        wq, wqr, wk, wv = _prep_mla(mla_wuq[l], mla_wukv[l])
        qd, kd, vd = _mla_prep(cq, ckv, small, cos_t, sin_t, mla_gq[l][None], mla_gkv[l][None],
                               wq, wqr, wk, wv, pk, pkr, seq=seq)
        y_d = _attention("mla", qd, 0, kd, 0, vd, batch=batch, seq=seq)
        h = _merge(h, norm1_g[l][None], (y_a, y_b, y_c, y_d), w_gate[l].astype(BF16), b_gate[l][:, None, :],
                   w_branch[l].astype(BF16), w_out[l].astype(BF16))

        wr = jnp.zeros((d, LANES), F32).at[:, :N_EXPERTS].set(w_router_expert[l]).at[
            :, N_EXPERTS:N_EXPERTS + N_GROUPS].set(w_router_group[l])
        br = jnp.zeros((1, LANES), F32).at[0, :N_EXPERTS].set(b_router_expert[l]).at[
            0, N_EXPERTS:N_EXPERTS + N_GROUPS].set(b_router_group[l])
        hn, ri, rw, cnt = _route(h, norm2_g[l][None], wr, br)
        counts = cnt[0, :N_EXPERTS].astype(I32)
        starts = jnp.cumsum(counts) - counts
        pcounts = ((counts + FFN_ROWS - 1) // FFN_ROWS) * FFN_ROWS
        pends = jnp.cumsum(pcounts)
        pstarts = pends - pcounts
        order = jnp.argsort(ri[:, 0:2].T.reshape(-1)).astype(I32)
        blk_start = jnp.arange(n_blk, dtype=I32) * FFN_ROWS
        blk_e = jnp.minimum(jnp.sum(blk_start[:, None] >= pends[None, :], axis=1), N_EXPERTS - 1).astype(I32)
        pos = jnp.arange(p_rows, dtype=I32)
        e_p = jnp.repeat(blk_e, FFN_ROWS)
        off = pos - pstarts[e_p]
        real = off < counts[e_p]
        before = starts[e_p] + jnp.minimum(off, counts[e_p])
        slot = order[jnp.minimum(before, n_slots - 1)]
        src_tok = jnp.where(real, slot % t, 0)
        dst_row = jnp.where(real, slot, n_slots + pos - before)
        w13 = jnp.concatenate([w_exp_gate[l], w_exp_up[l]], axis=-1).astype(BF16)
        y = _expert_ffn(hn, src_tok.reshape(n_blk, 1, FFN_ROWS), dst_row.reshape(n_blk, 1, FFN_ROWS), blk_e,
                        w13, w_exp_down[l].astype(BF16), y_rows=p_rows)
        moe = (y, rw)
    return _final_norm(h, *moe, final_g[None]).reshape(batch, seq, d)
```

```python
import functools

import numpy as np
import jax
import jax.numpy as jnp
from jax import lax
from jax.experimental import pallas as pl
from jax.experimental.pallas import tpu as pltpu

F32 = jnp.float32
BF16 = jnp.bfloat16
I32 = jnp.int32

HEAD_DIM = 64
N_HEADS = 4
ATT_W = N_HEADS * HEAD_DIM
MOBA_BLOCK = 256
MOBA_TOPK = 3
CONV_CH = 256
CONV_K = 31
MLA_Q_LORA = 256
MLA_KV_LORA = 128
MLA_NOPE = 64
MLA_ROPE = 32
MLA_V = 64
MLA_HEAD_PAD = 128
ROPE_THETA = 10000.0
BRANCH_W = 256
N_BRANCH = 4
N_GROUPS = 4
EXPERTS_PER_GROUP = 8
N_EXPERTS = N_GROUPS * EXPERTS_PER_GROUP
EXPERT_FF = 256
NORM_EPS = 1e-6
NEG_INF = -1e30
LOG2E = 1.4426950408889634

LANES = 128
SUBLANES = 8
VMEM_LIMIT = 56 * 1024 * 1024

TOKEN_TILE = 512
ATT_TK = 256
V_ROWS = 80
ATT_UNROLL = 4
CONV_CHUNK = 128
FFN_ROWS = 256

C_MOBA, C_FOX, C_CONV, C_CQ, C_CKV, C_SMALL, C_END = 0, 768, 1536, 2048, 2304, 2432, 2560
SM_ROPE = 0
SM_CUM = 32


def _cparams(sem):
    return pltpu.CompilerParams(dimension_semantics=sem, vmem_limit_bytes=VMEM_LIMIT)


def _rms(x, g):
    return x * lax.rsqrt(jnp.mean(x * x, axis=-1, keepdims=True) + NORM_EPS) * g


def _sigmoid(x):
    return 1.0 / (1.0 + jnp.exp(-x))


def _dot(a, b):
    return jnp.dot(a, b, preferred_element_type=F32)


def _values_t(v):
    vt = v.T
    ones = jnp.ones((V_ROWS - HEAD_DIM, vt.shape[1]), F32)
    parts = [x for h in range(N_HEADS) for x in (vt[h * HEAD_DIM:(h + 1) * HEAD_DIM, :], ones)]
    return jnp.concatenate(parts, axis=0).astype(BF16)


def _moe_combine(h_ref, moe_refs):
    ya_ref, yb_ref, rw_ref = moe_refs
    rw = rw_ref[...]
    return h_ref[...] + rw[:, 0:1] * _load_row_tiles(ya_ref) + rw[:, 1:2] * _load_row_tiles(yb_ref)


def _inproj_kernel(*refs, tiles_per_seq, combine):
    if combine:
        h_ref, *moe_refs, g_ref, w_ref, bf_ref, hout_ref = refs[:8]
        refs = refs[8:]
        h = _moe_combine(h_ref, moe_refs)
        hout_ref[...] = h
    else:
        h_ref, g_ref, w_ref, bf_ref = refs[:4]
        refs = refs[4:]
        h = h_ref[...]
    moba_ref, mvt_ref, fox_ref, fvt_ref, conv_ref, cq_ref, ckv_ref, small_ref, cumt_ref, carry_ref = refs
    i = pl.program_id(0)
    tm = h_ref.shape[0]
    xn = _rms(h, g_ref[...]).astype(BF16)

    def proj(a, b):
        return _dot(xn, w_ref[:, a:b])

    for qk_ref, vt_ref, c0 in ((moba_ref, mvt_ref, C_MOBA), (fox_ref, fvt_ref, C_FOX)):
        qk_ref[:, :ATT_W] = (proj(c0, c0 + ATT_W) * (HEAD_DIM ** -0.5 * LOG2E)).astype(BF16)
        qk_ref[:, ATT_W:] = proj(c0 + ATT_W, c0 + 2 * ATT_W).astype(BF16)
        vt_ref[0] = _values_t(proj(c0 + 2 * ATT_W, c0 + 3 * ATT_W))
    conv_ref[...] = proj(C_CONV, C_CQ).astype(BF16)
    cq_ref[...] = proj(C_CQ, C_CKV).astype(BF16)
    ckv_ref[...] = proj(C_CKV, C_SMALL).astype(BF16)
    sm = proj(C_SMALL, C_END)

    f = sm + bf_ref[...]
    c = jnp.minimum(f, 0.0) - jnp.log(1.0 + jnp.exp(-jnp.abs(f)))
    row = lax.broadcasted_iota(I32, c.shape, 0)
    sh = 1
    while sh < tm:
        c = c + jnp.where(row >= sh, pltpu.roll(c, sh, 0), 0.0)
        sh *= 2

    @pl.when(i % tiles_per_seq == 0)
    def _():
        carry_ref[...] = jnp.zeros_like(carry_ref)

    c = c + carry_ref[...]
    carry_ref[...] = c[tm - 1:tm, :]
    lane = lax.broadcasted_iota(I32, c.shape, 1)
    small_ref[...] = jnp.where((lane >= SM_CUM) & (lane < SM_CUM + N_HEADS), c, sm)
    cumt_ref[0] = c.T[SM_CUM:SM_CUM + SUBLANES, :]


def _moe_specs(t, tm, d):
    assert d == ROW_TILE * LANES
    return [pl.BlockSpec((tm * ROW_TILE, LANES), lambda i: (i, 0)),
            pl.BlockSpec((tm * ROW_TILE, LANES), lambda i: (i + t // tm, 0)),
            pl.BlockSpec((tm, LANES), lambda i: (i, 0))]


def _inproj(h, g, w, bf, *, seq, moe=None):
    t, d = h.shape
    tm = min(TOKEN_TILE, seq)
    tiles_per_seq = seq // tm
    nb = t // seq
    row = lambda i: (i, 0)
    fixed = lambda i: (0, 0)
    seq_t = lambda i: (i // tiles_per_seq, 0, i % tiles_per_seq)
    tok = lambda wdt, dt: (pl.BlockSpec((tm, wdt), row), jax.ShapeDtypeStruct((t, wdt), dt))
    chan = lambda c, dt: (pl.BlockSpec((1, c, tm), seq_t), jax.ShapeDtypeStruct((nb, c, seq), dt))
    outs = [tok(2 * ATT_W, BF16), chan(N_HEADS * V_ROWS, BF16), tok(2 * ATT_W, BF16), chan(N_HEADS * V_ROWS, BF16),
            tok(C_CQ - C_CONV, BF16), tok(C_CKV - C_CQ, BF16), tok(C_SMALL - C_CKV, BF16),
            tok(C_END - C_SMALL, F32), chan(SUBLANES, F32)]
    if moe is not None:
        outs = [tok(d, F32)] + outs
    return pl.pallas_call(
        functools.partial(_inproj_kernel, tiles_per_seq=tiles_per_seq, combine=moe is not None),
        grid=(t // tm,),
        in_specs=[pl.BlockSpec((tm, d), row)] + (_moe_specs(t, tm, d) if moe is not None else [])
        + [pl.BlockSpec((1, d), fixed), pl.BlockSpec((d, C_END), fixed), pl.BlockSpec((1, LANES), fixed)],
        out_specs=[o[0] for o in outs],
        out_shape=[o[1] for o in outs],
        scratch_shapes=[pltpu.VMEM((1, LANES), F32)],
        compiler_params=_cparams(("arbitrary",)),
        name="inproj",
    )(h, *((moe[0], moe[0], moe[1]) if moe is not None else ()), g, w, bf)


def _attn_kernel(*refs, mode, tq, tk, nblk, n_sel):
    qs_ref, ua_ref, ub_ref, mba_ref, mbb_ref, m_ref, acc_ref = refs[-7:]
    refs = refs[:-7]
    if mode == "fox":
        q_ref, k_ref, vt_ref, small_ref, cumt_ref, o_ref, csb_ref = refs
    elif mode == "moba":
        q_ref, k_ref, vt_ref, o_ref, kmean_ref, selb_ref = refs
    else:
        q_ref, k_ref, vt_ref, o_ref = refs
    i = pl.program_id(1)
    nh = N_HEADS
    rows = nh * tq
    seq = k_ref.shape[1]
    n_full = (i * tq) // tk
    q0 = i * tq

    q = q_ref[0]
    if mode == "mla":
        for h in range(nh):
            qs_ref[:, h * tq:(h + 1) * tq] = q[:, h * MLA_HEAD_PAD:(h + 1) * MLA_HEAD_PAD].astype(F32).T.astype(BF16)
    else:
        qt = q.astype(F32).T.astype(BF16)
        sub_h = lax.shift_right_logical(lax.broadcasted_iota(I32, (ATT_W, tq), 0), 6)
        for h in range(nh):
            qs_ref[:, h * tq:(h + 1) * tq] = jnp.where(sub_h == h, qt, jnp.zeros_like(qt))

    r_in = lax.broadcasted_iota(I32, (tk, rows), 1) & (tq - 1)
    c_in = lax.broadcasted_iota(I32, (tk, rows), 0)

    def per_head_rows(x):
        return jnp.concatenate(
            [jnp.broadcast_to(x[:, h * tq:(h + 1) * tq], (V_ROWS, tq)) for h in range(nh)], axis=0)

    if mode == "fox":
        @pl.when(i == 0)
        def _():
            def fill(c, _):
                r0 = pl.multiple_of(c * tk, tk)
                cs = small_ref[0, pl.ds(r0, tk), :]
                for h in range(nh):
                    csb_ref[pl.ds(r0, tk), h * LANES:(h + 1) * LANES] = jnp.broadcast_to(
                        -LOG2E * cs[:, SM_CUM + h:SM_CUM + h + 1], (tk, LANES))
                return 0
            lax.fori_loop(0, seq // tk, fill, 0)

        ct = cumt_ref[0] * LOG2E
        c_t = jnp.concatenate([ct[h:h + 1, :] for h in range(nh)], axis=1)
    if mode == "moba":
        slopes = [LOG2E * 2.0 ** (-8.0 * (h + 1) / nh) for h in range(nh)]
        slope_row = jnp.concatenate([jnp.full((1, tq), s, F32) for s in slopes], axis=1)
        alibi_key = slope_row * c_in.astype(F32)
        r_row = (lax.broadcasted_iota(I32, (1, rows), 1) & (tq - 1)).astype(F32)
        nbp = kmean_ref.shape[0]

        @pl.when(i == 0)
        def _():
            kmean_ref[...] = jnp.zeros_like(kmean_ref)
            for n in range(nblk):
                kb = k_ref[0, n * MOBA_BLOCK:(n + 1) * MOBA_BLOCK, :].astype(F32)
                kmean_ref[n:n + 1, :] = jnp.mean(kb, axis=0, keepdims=True)

        km = kmean_ref[...]
        km_hi = km.astype(BF16)
        km_lo = (km - km_hi.astype(F32)).astype(BF16)
        cur = (i * tq) // MOBA_BLOCK
        qs = qs_ref[...]
        gate = _dot(km_hi, qs) + _dot(km_lo, qs)
        n_iota = lax.broadcasted_iota(I32, gate.shape, 0)
        gate = jnp.where(n_iota < cur, gate, NEG_INF)
        rank = jnp.zeros(gate.shape, I32)
        for n2 in range(nblk):
            gb = gate[n2:n2 + 1, :]
            ahead = jnp.where(gb > gate, 1, jnp.where((gb == gate) & (n_iota > n2), 1, 0))
            rank = rank + ahead
        selb_ref[...] = jnp.where((rank < n_sel) & (n_iota < cur), 0.0, NEG_INF)

    def produce(u_ref, mb_ref, j, diagonal=False):
        k0 = pl.multiple_of(j * tk, tk)
        kj = k_ref[0, pl.ds(k0, tk), :]
        if mode == "mla":
            u = jnp.concatenate(
                [_dot(kj[:, h * MLA_HEAD_PAD:(h + 1) * MLA_HEAD_PAD], qs_ref[:, h * tq:(h + 1) * tq])
                 for h in range(nh)], axis=1)
        else:
            u = _dot(kj, qs_ref[...])
        if mode == "fox":
            cs = csb_ref[pl.ds(k0, tk), :]
            u = u + jnp.concatenate([cs[:, h * LANES:(h + 1) * LANES] for h in range(nh)
                                     for _ in range(tq // LANES)], axis=1)
        if mode == "moba":
            u = u + alibi_key
        if diagonal:
            u = jnp.where(c_in + j * tk <= r_in + q0, u, NEG_INF)
        u_ref[...] = u
        mb_ref[...] = jnp.max(u, axis=0, keepdims=True)

    def consume(u_ref, mb_ref, j, diagonal=False):
        vtj = vt_ref[0, :, pl.ds(pl.multiple_of(j * tk, tk), tk)]
        m = m_ref[...]
        m_blk = mb_ref[...]
        if mode == "fox":
            row = c_t
        if mode == "moba":
            row = slope_row * ((j * tk - q0).astype(F32) - r_row)
            if not diagonal:
                row = row + selb_ref[pl.ds(j, 1), :]
        if mode != "mla":
            m_blk = m_blk + row
        m_new = jnp.maximum(m, m_blk)
        alpha = jnp.exp2(m - m_new)
        shift = m_new - row if mode != "mla" else m_new
        os_ = []
        for h in range(nh):
            cols = slice(h * tq, (h + 1) * tq)
            p = jnp.exp2(u_ref[:, cols] - shift[:, cols])
            os_.append(_dot(vtj[h * V_ROWS:(h + 1) * V_ROWS, :], p.astype(BF16)))
        m_ref[...] = m_new
        acc_ref[...] = per_head_rows(alpha) * acc_ref[...] + jnp.concatenate(os_, axis=0)

    m_ref[...] = jnp.full(m_ref.shape, NEG_INF, F32)
    acc_ref[...] = jnp.zeros(acc_ref.shape, F32)
    produce(ub_ref, mbb_ref, n_full, True)
    produce(ua_ref, mba_ref, 0)
    consume(ub_ref, mbb_ref, n_full, True)

    bufs = ((ua_ref, mba_ref), (ub_ref, mbb_ref))

    def trip(t, _):
        for k in range(ATT_UNROLL):
            produce(*bufs[(k + 1) % 2], ATT_UNROLL * t + k + 1)
            consume(*bufs[k % 2], ATT_UNROLL * t + k)
        return 0

    lax.fori_loop(0, n_full // ATT_UNROLL, trip, 0)
    base = (n_full // ATT_UNROLL) * ATT_UNROLL
    for k in range(ATT_UNROLL - 1):
        @pl.when(n_full - base > k)
        def _():
            if k + 1 < ATT_UNROLL - 1:
                produce(*bufs[(k + 1) % 2], jnp.minimum(base + k + 1, n_full))
            consume(*bufs[k % 2], base + k)

    acc = acc_ref[...]
    out_t = jnp.concatenate(
        [acc[h * V_ROWS:h * V_ROWS + HEAD_DIM, :] * (1.0 / acc[h * V_ROWS + HEAD_DIM:h * V_ROWS + HEAD_DIM + 1, :])
         for h in range(nh)], axis=0)
    o_ref[0] = out_t.T.astype(BF16)


def _attention(mode, q_arr, q_blk, k_arr, k_blk, vt_arr, *, batch, seq, extra=()):
    tq = tk = min(ATT_TK, seq)
    assert tq % LANES == 0 and seq % tk == 0
    dq = MLA_HEAD_PAD * N_HEADS if mode == "mla" else ATT_W
    nblk = seq // MOBA_BLOCK
    rows = N_HEADS * tq
    r3 = lambda a: a.reshape(batch, seq, a.shape[-1])
    in_specs = [pl.BlockSpec((1, tq, dq), lambda b, i: (b, i, q_blk)),
                pl.BlockSpec((1, seq, dq), lambda b, i: (b, 0, k_blk)),
                pl.BlockSpec((1, N_HEADS * V_ROWS, seq), lambda b, i: (b, 0, 0))]
    args = [r3(q_arr), r3(k_arr), vt_arr]
    scratch = []
    if mode == "fox":
        small, cumt = extra
        in_specs += [pl.BlockSpec((1, seq, LANES), lambda b, i: (b, 0, 0)),
                     pl.BlockSpec((1, SUBLANES, tq), lambda b, i: (b, 0, i))]
        args += [r3(small), cumt]
        scratch = [pltpu.VMEM((seq, N_HEADS * LANES), F32)]
    if mode == "moba":
        nbp = -(-nblk // SUBLANES) * SUBLANES
        scratch = [pltpu.VMEM((nbp, ATT_W), F32), pltpu.VMEM((nbp, rows), F32)]
    scratch += [pltpu.VMEM((dq // N_HEADS if mode == "mla" else ATT_W, rows), BF16),
                pltpu.VMEM((tk, rows), F32), pltpu.VMEM((tk, rows), F32),
                pltpu.VMEM((1, rows), F32), pltpu.VMEM((1, rows), F32),
                pltpu.VMEM((1, rows), F32),
                pltpu.VMEM((N_HEADS * V_ROWS, tq), F32)]
    out = pl.pallas_call(
        functools.partial(_attn_kernel, mode=mode, tq=tq, tk=tk, nblk=nblk, n_sel=min(MOBA_TOPK, nblk - 1)),
        grid=(batch, seq // tq),
        in_specs=in_specs,
        out_specs=pl.BlockSpec((1, tq, ATT_W), lambda b, i: (b, i, 0)),
        out_shape=jax.ShapeDtypeStruct((batch, seq, ATT_W), BF16),
        scratch_shapes=scratch,
        compiler_params=_cparams(("parallel", "arbitrary")),
        name="attn_" + mode,
    )(*args)
    return out.reshape(batch * seq, ATT_W)


def _conv_kernel(x_ref, w_ref, p_ref, o_ref, u_ref, *, chunk):
    seq = x_ref.shape[1]
    halo = 32
    u_ref[0:halo, :] = jnp.zeros((halo, CONV_CH), F32)

    def glu(c, _):
        r0 = pl.multiple_of(c * chunk, chunk)
        x = x_ref[0, pl.ds(r0, chunk), :].astype(F32)
        u_ref[pl.ds(halo + r0, chunk), :] = x[:, :CONV_CH] * _sigmoid(x[:, CONV_CH:])
        return 0

    lax.fori_loop(0, seq // chunk, glu, 0)
    bias, ln_g, ln_b = p_ref[0:1, :], p_ref[1:2, :], p_ref[2:3, :]

    def conv(c, _):
        r0 = pl.multiple_of(c * chunk, chunk)
        n = chunk + halo
        win = u_ref[pl.ds(r0, n), :]
        acc = jnp.zeros((chunk, CONV_CH), F32)
        for b in range(SUBLANES):
            shifted = win if b == 0 else pltpu.roll(win, n - b, 0)
            for a in range(halo // SUBLANES + 1):
                k = SUBLANES * a + b - (halo - (CONV_K - 1))
                if 0 <= k < CONV_K:
                    acc = acc + w_ref[k:k + 1, :] * shifted[SUBLANES * a:SUBLANES * a + chunk, :]
        acc = acc + bias
        mu = jnp.mean(acc, axis=-1, keepdims=True)
        xc = acc - mu
        var = jnp.mean(xc * xc, axis=-1, keepdims=True)
        y = xc * lax.rsqrt(var + NORM_EPS) * ln_g + ln_b
        o_ref[0, pl.ds(r0, chunk), :] = (y * _sigmoid(y)).astype(BF16)
        return 0

    lax.fori_loop(0, seq // chunk, conv, 0)


def _conv_module(conv_in, w, p, *, batch, seq):
    chunk = min(CONV_CHUNK, seq)
    out = pl.pallas_call(
        functools.partial(_conv_kernel, chunk=chunk),
        grid=(batch,),
        in_specs=[pl.BlockSpec((1, seq, 2 * CONV_CH), lambda b: (b, 0, 0)),
                  pl.BlockSpec((32, CONV_CH), lambda b: (0, 0)),
                  pl.BlockSpec((SUBLANES, CONV_CH), lambda b: (0, 0))],
        out_specs=pl.BlockSpec((1, seq, CONV_CH), lambda b: (b, 0, 0)),
        out_shape=jax.ShapeDtypeStruct((batch, seq, CONV_CH), BF16),
        scratch_shapes=[pltpu.VMEM((seq + 32, CONV_CH), F32)],
        compiler_params=_cparams(("parallel",)),
        name="conv",
    )(conv_in.reshape(batch, seq, 2 * CONV_CH), w, p)
    return out.reshape(batch * seq, CONV_CH)


def _mla_prep_kernel(cq_ref, ckv_ref, small_ref, cos_ref, sin_ref, gq_ref, gkv_ref, wq_ref, wqr_ref,
                     wk_ref, wv_ref, pk_ref, pkr_ref, q_ref, k_ref, v_ref, *, scale):
    cos, sin = cos_ref[...], sin_ref[...]
    cq = _rms(cq_ref[...].astype(F32), gq_ref[...]).astype(BF16)
    q = _dot(cq, wq_ref[...]) * cos + _dot(cq, wqr_ref[...]) * sin
    q_ref[...] = (q * scale).astype(BF16)
    ckv = _rms(ckv_ref[...].astype(F32), gkv_ref[...]).astype(BF16)
    kr = small_ref[...].astype(BF16)
    k = _dot(ckv, wk_ref[...]) + _dot(kr, pk_ref[...]) * cos + _dot(kr, pkr_ref[...]) * sin
    k_ref[...] = k.astype(BF16)
    v_ref[0] = _values_t(_dot(ckv, wv_ref[...]))


def _mla_prep(cq, ckv, small, cos_t, sin_t, gq, gkv, wq, wqr, wk, wv, pk, pkr, *, seq):
    t = cq.shape[0]
    tm = min(TOKEN_TILE, seq)
    tiles_per_seq = seq // tm
    wfull = MLA_HEAD_PAD * N_HEADS
    row = lambda i: (i, 0)
    pos = lambda i: (i % tiles_per_seq, 0)
    fixed = lambda i: (0, 0)
    return pl.pallas_call(
        functools.partial(_mla_prep_kernel, scale=float((MLA_NOPE + MLA_ROPE) ** -0.5 * LOG2E)),
        grid=(t // tm,),
        in_specs=[pl.BlockSpec((tm, MLA_Q_LORA), row), pl.BlockSpec((tm, MLA_KV_LORA), row),
                  pl.BlockSpec((tm, LANES), row),
                  pl.BlockSpec((tm, wfull), pos), pl.BlockSpec((tm, wfull), pos),
                  pl.BlockSpec((1, MLA_Q_LORA), fixed), pl.BlockSpec((1, MLA_KV_LORA), fixed),
                  pl.BlockSpec((MLA_Q_LORA, wfull), fixed), pl.BlockSpec((MLA_Q_LORA, wfull), fixed),
                  pl.BlockSpec((MLA_KV_LORA, wfull), fixed), pl.BlockSpec((MLA_KV_LORA, ATT_W), fixed),
                  pl.BlockSpec((LANES, wfull), fixed), pl.BlockSpec((LANES, wfull), fixed)],
        out_specs=[pl.BlockSpec((tm, wfull), row), pl.BlockSpec((tm, wfull), row),
                   pl.BlockSpec((1, N_HEADS * V_ROWS, tm), lambda i: (i // tiles_per_seq, 0, i % tiles_per_seq))],
        out_shape=[jax.ShapeDtypeStruct((t, wfull), BF16), jax.ShapeDtypeStruct((t, wfull), BF16),
                   jax.ShapeDtypeStruct((t // seq, N_HEADS * V_ROWS, seq), BF16)],
        compiler_params=_cparams(("parallel",)),
        name="mla_prep",
    )(cq, ckv, small, cos_t, sin_t, gq, gkv, wq, wqr, wk, wv, pk, pkr)


def _merge_kernel(h_ref, g_ref, ya_ref, yb_ref, yc_ref, yd_ref, wg_ref, bg_ref, wb_ref, wo_ref, o_ref):
    h = h_ref[...]
    xn = _rms(h, g_ref[...]).astype(BF16)
    merged = jnp.zeros(h.shape, F32)
    for i, y_ref in enumerate((ya_ref, yb_ref, yc_ref, yd_ref)):
        gate = _sigmoid(_dot(xn, wg_ref[i]) + bg_ref[i])
        merged = merged + gate * _dot(y_ref[...], wb_ref[i])
    o_ref[...] = h + _dot(merged.astype(BF16), wo_ref[...])


def _merge(h, g, ys, wg, bg, wb, wo):
    t, d = h.shape
    tm = min(TOKEN_TILE, t)
    row = lambda i: (i, 0)
    fixed2 = lambda i: (0, 0)
    fixed3 = lambda i: (0, 0, 0)
    return pl.pallas_call(
        _merge_kernel,
        grid=(t // tm,),
        in_specs=[pl.BlockSpec((tm, d), row), pl.BlockSpec((1, d), fixed2)]
        + [pl.BlockSpec((tm, BRANCH_W), row)] * N_BRANCH
        + [pl.BlockSpec((N_BRANCH, d, d), fixed3), pl.BlockSpec((N_BRANCH, 1, d), fixed3),
           pl.BlockSpec((N_BRANCH, BRANCH_W, d), fixed3), pl.BlockSpec((d, d), fixed2)],
        out_specs=pl.BlockSpec((tm, d), row),
        out_shape=jax.ShapeDtypeStruct((t, d), F32),
        compiler_params=_cparams(("parallel",)),
        name="merge",
    )(h, g, *ys, wg, bg, wb, wo)


ROW_TILE = 8


def _store_row_tiles(ref, x):
    m = x.shape[0]
    for j in range(ROW_TILE):
        ref[pl.ds(j, m, stride=ROW_TILE), :] = x[:, j * LANES:(j + 1) * LANES]


def _load_row_tiles(ref):
    m = ref.shape[0] // ROW_TILE
    return jnp.concatenate([ref[pl.ds(j, m, stride=ROW_TILE), :] for j in range(ROW_TILE)], axis=1)


def _route_kernel(h_ref, g_ref, wr_ref, br_ref, hn_ref, ri_ref, rw_ref, cnt_ref, carry_ref):
    i = pl.program_id(0)
    hn = _rms(h_ref[...], g_ref[...])
    _store_row_tiles(hn_ref, hn)
    hi = hn.astype(BF16)
    lo = (hn - hi.astype(F32)).astype(BF16)
    w = wr_ref[...]
    w_hi = w.astype(BF16)
    w_lo = (w - w_hi.astype(F32)).astype(BF16)
    lg = _dot(hi, w_hi) + _dot(lo, w_hi) + _dot(hi, w_lo) + br_ref[...]
    lane = lax.broadcasted_iota(I32, lg.shape, 1)
    big = jnp.int32(4 * LANES)

    def top1(vals):
        mx = jnp.max(vals, axis=1, keepdims=True)
        idx = jnp.min(jnp.where(vals == mx, lane, big), axis=1, keepdims=True)
        return mx, idx

    is_g = (lane >= N_EXPERTS) & (lane < N_EXPERTS + N_GROUPS)
    gmax, gidx = top1(jnp.where(is_g, lg, NEG_INF))
    p_sel = 1.0 / jnp.sum(jnp.where(is_g, jnp.exp(lg - gmax), 0.0), axis=1, keepdims=True)
    lo = (gidx - N_EXPERTS) * EXPERTS_PER_GROUP
    el = jnp.where((lane >= lo) & (lane < lo + EXPERTS_PER_GROUP), lg, NEG_INF)
    l1, i1 = top1(el)
    l2, i2 = top1(jnp.where(lane == i1, NEG_INF, el))
    e = jnp.exp(l2 - l1)
    w1 = p_sel / (1.0 + e)
    w2 = p_sel * e / (1.0 + e)

    @pl.when(i == 0)
    def _():
        carry_ref[...] = jnp.zeros_like(carry_ref)

    onehot = jnp.where((lane == i1) | (lane == i2), 1.0, 0.0)
    carry_ref[...] = carry_ref[...] + jnp.sum(onehot, axis=0, keepdims=True)
    cnt_ref[...] = carry_ref[...]
    ri_ref[...] = jnp.where(lane == 0, i1, jnp.where(lane == 1, i2, 0))
    rw_ref[...] = jnp.where(lane == 0, w1, jnp.where(lane == 1, w2, 0.0))


def _route(h, g, wr, br):
    t, d = h.shape
    tm = min(TOKEN_TILE, t)
    row = lambda i: (i, 0)
    fixed = lambda i: (0, 0)
    return pl.pallas_call(
        _route_kernel,
        grid=(t // tm,),
        in_specs=[pl.BlockSpec((tm, d), row), pl.BlockSpec((1, d), fixed),
                  pl.BlockSpec((d, LANES), fixed), pl.BlockSpec((1, LANES), fixed)],
        out_specs=[pl.BlockSpec((tm * ROW_TILE, LANES), row), pl.BlockSpec((tm, LANES), row),
                   pl.BlockSpec((tm, LANES), row), pl.BlockSpec((1, LANES), fixed)],
        out_shape=[jax.ShapeDtypeStruct((t * ROW_TILE, LANES), F32), jax.ShapeDtypeStruct((t, LANES), I32),
                   jax.ShapeDtypeStruct((t, LANES), F32), jax.ShapeDtypeStruct((1, LANES), F32)],
        scratch_shapes=[pltpu.VMEM((1, LANES), F32)],
        compiler_params=_cparams(("arbitrary",)),
        name="route",
    )(h, g, wr, br)


def _ffn_kernel(blk_e_ref, src0_ref, src1_ref, src2_ref, dst0_ref, dst1_ref, hn_hbm, w13_ref, w2_ref, y_hbm,
                src_s, dst_s, xa, xb, oa, ob, sem_idx, sem_in, sem_out):
    i = pl.program_id(0)
    n = pl.num_programs(0)
    xbufs, obufs = (xa, xb), (oa, ob)
    rows = xa.shape[0] // ROW_TILE

    def src_copy(vmem_ref, p):
        return pltpu.make_async_copy(vmem_ref.at[0, 0], src_s.at[p], sem_idx.at[0])

    def dst_copy(vmem_ref, p):
        return pltpu.make_async_copy(vmem_ref.at[0, 0], dst_s.at[p], sem_idx.at[1])

    def tile(ref, r):
        return ref.at[pl.ds(pl.multiple_of(r * ROW_TILE, ROW_TILE), ROW_TILE)]

    def start_gather(s):
        for r in range(rows):
            pltpu.make_async_copy(tile(hn_hbm, src_s[s, r]), tile(xbufs[s], r), sem_in.at[s]).start()

    def start_scatter(s):
        for r in range(rows):
            pltpu.make_async_copy(tile(obufs[s], r), tile(y_hbm, dst_s[s, r]), sem_out.at[s]).start()

    def wait_gather(s):
        pltpu.make_async_copy(hn_hbm.at[pl.ds(0, rows * ROW_TILE)], xbufs[s], sem_in.at[s]).wait()

    def wait_scatter(s):
        pltpu.make_async_copy(obufs[s], y_hbm.at[pl.ds(0, rows * ROW_TILE)], sem_out.at[s]).wait()

    @pl.when(i == 0)
    def _():
        first = src_copy(src0_ref, 0)
        first.start()
        first.wait()
        start_gather(0)
        second, third = src_copy(src1_ref, 1), dst_copy(dst0_ref, 0)
        second.start()
        third.start()
        second.wait()
        third.wait()

    def block(s):
        @pl.when(i + 1 < n)
        def _():
            start_gather(1 - s)

        nxt_src, nxt_dst = src_copy(src2_ref, s), dst_copy(dst1_ref, 1 - s)
        nxt_src.start()
        nxt_dst.start()
        wait_gather(s)

        @pl.when(i >= 2)
        def _():
            wait_scatter(s)

        x = _load_row_tiles(xbufs[s]).astype(BF16)
        ab = _dot(x, w13_ref[0])
        a, b = ab[:, :EXPERT_FF], ab[:, EXPERT_FF:]
        act = (a * _sigmoid(a) * b).astype(BF16)
        _store_row_tiles(obufs[s], _dot(act, w2_ref[0]))
        start_scatter(s)

        nxt_src.wait()
        nxt_dst.wait()

        @pl.when(i == n - 1)
        def _():
            wait_scatter(s)
            wait_scatter(1 - s)

    for s in range(2):
        pl.when(i % 2 == s)(functools.partial(block, s))


def _expert_ffn(hn, src_tok, dst_row, blk_e, w13, w2, *, y_rows):
    d = ROW_TILE * LANES
    n_blk = src_tok.shape[0]
    assert n_blk >= 2
    idx_blk = (1, 1, FFN_ROWS)
    tile_buf = pltpu.VMEM((FFN_ROWS * ROW_TILE, LANES), F32)
    grid_spec = pltpu.PrefetchScalarGridSpec(
        num_scalar_prefetch=1,
        grid=(n_blk,),
        in_specs=[pl.BlockSpec(idx_blk, lambda i, be: (i, 0, 0)),
                  pl.BlockSpec(idx_blk, lambda i, be: (jnp.minimum(i + 1, n_blk - 1), 0, 0)),
                  pl.BlockSpec(idx_blk, lambda i, be: (jnp.minimum(i + 2, n_blk - 1), 0, 0)),
                  pl.BlockSpec(idx_blk, lambda i, be: (i, 0, 0)),
                  pl.BlockSpec(idx_blk, lambda i, be: (jnp.minimum(i + 1, n_blk - 1), 0, 0)),
                  pl.BlockSpec(memory_space=pl.ANY),
                  pl.BlockSpec((1, d, 2 * EXPERT_FF), lambda i, be: (be[i], 0, 0)),
                  pl.BlockSpec((1, EXPERT_FF, d), lambda i, be: (be[i], 0, 0))],
        out_specs=pl.BlockSpec(memory_space=pl.ANY),
        scratch_shapes=[pltpu.SMEM((2, FFN_ROWS), I32), pltpu.SMEM((2, FFN_ROWS), I32),
                        tile_buf, tile_buf, tile_buf, tile_buf,
                        pltpu.SemaphoreType.DMA((2,)), pltpu.SemaphoreType.DMA((2,)),
                        pltpu.SemaphoreType.DMA((2,))],
    )
    return pl.pallas_call(
        _ffn_kernel,
        grid_spec=grid_spec,
        out_shape=jax.ShapeDtypeStruct((y_rows * ROW_TILE, LANES), F32),
        compiler_params=_cparams(("arbitrary",)),
        name="expert_ffn",
    )(blk_e, src_tok, src_tok, src_tok, dst_row, dst_row, hn, w13, w2)


def _final_kernel(h_ref, ya_ref, yb_ref, rw_ref, g_ref, o_ref):
    o_ref[...] = _rms(_moe_combine(h_ref, (ya_ref, yb_ref, rw_ref)), g_ref[...])


def _final_norm(h, y, rw, g):
    t, d = h.shape
    tm = min(TOKEN_TILE, t)
    return pl.pallas_call(
        _final_kernel,
        grid=(t // tm,),
        in_specs=[pl.BlockSpec((tm, d), lambda i: (i, 0))] + _moe_specs(t, tm, d)
        + [pl.BlockSpec((1, d), lambda i: (0, 0))],
        out_specs=pl.BlockSpec((tm, d), lambda i: (i, 0)),
        out_shape=jax.ShapeDtypeStruct((t, d), F32),
        compiler_params=_cparams(("parallel",)),
        name="final_norm",
    )(h, y, y, rw, g)


def _prep_inproj(w_in, b_forget):
    d = w_in.shape[0]
    o = np.cumsum([0, 3 * ATT_W, 3 * ATT_W, N_HEADS, 2 * CONV_CH, MLA_Q_LORA, MLA_KV_LORA, MLA_ROPE]).tolist()
    parts = [w_in[:, o[0]:o[1]], w_in[:, o[1]:o[2]], w_in[:, o[3]:o[4]],
             w_in[:, o[4]:o[5]], w_in[:, o[5]:o[6]], w_in[:, o[6]:o[7]], w_in[:, o[2]:o[3]],
             jnp.zeros((d, C_END - C_SMALL - MLA_ROPE - N_HEADS), F32)]
    w = jnp.concatenate(parts, axis=1).astype(BF16)
    bf = jnp.zeros((1, LANES), F32).at[0, SM_CUM:SM_CUM + N_HEADS].set(b_forget)
    return w, bf


def _rope_placement():
    hp, half = MLA_HEAD_PAD, MLA_ROPE // 2
    pk = np.zeros((LANES, hp * N_HEADS), np.float32)
    pkr = np.zeros((LANES, hp * N_HEADS), np.float32)
    for h in range(N_HEADS):
        for i in range(MLA_ROPE):
            pk[SM_ROPE + i, h * hp + MLA_NOPE + i] = 1.0
        for i in range(half):
            pkr[SM_ROPE + half + i, h * hp + MLA_NOPE + i] = -1.0
            pkr[SM_ROPE + i, h * hp + MLA_NOPE + half + i] = 1.0
    return jnp.asarray(pk, BF16), jnp.asarray(pkr, BF16)


def _prep_mla(wuq, wukv):
    hp, half = MLA_HEAD_PAD, MLA_ROPE // 2
    qd = MLA_NOPE + MLA_ROPE
    zq = jnp.zeros((MLA_Q_LORA, hp - qd), F32)
    wq, wqr, wk, wv = [], [], [], []
    for h in range(N_HEADS):
        nope = wuq[:, h * qd:h * qd + MLA_NOPE]
        pe = wuq[:, h * qd + MLA_NOPE:(h + 1) * qd]
        wq += [nope, pe, zq]
        wqr += [jnp.zeros_like(nope), -pe[:, half:], pe[:, :half], zq]
        kv = wukv[:, h * (MLA_NOPE + MLA_V):(h + 1) * (MLA_NOPE + MLA_V)]
        wk += [kv[:, :MLA_NOPE], jnp.zeros((MLA_KV_LORA, hp - MLA_NOPE), F32)]
        wv += [kv[:, MLA_NOPE:]]
    cat = lambda xs: jnp.concatenate(xs, axis=1).astype(BF16)
    return cat(wq), cat(wqr), cat(wk), cat(wv)


def _rope_tables(seq):
    inv_freq = ROPE_THETA ** (-jnp.arange(0, MLA_ROPE, 2, dtype=F32) / MLA_ROPE)
    ang = jnp.arange(seq, dtype=F32)[:, None] * inv_freq[None, :]
    cos, sin = jnp.cos(ang), jnp.sin(ang)
    one = jnp.ones((seq, MLA_NOPE), F32)
    zero = jnp.zeros((seq, MLA_NOPE), F32)
    pad = jnp.zeros((seq, MLA_HEAD_PAD - MLA_NOPE - MLA_ROPE), F32)
    cos_h = jnp.concatenate([one, cos, cos, pad], axis=1)
    sin_h = jnp.concatenate([zero, sin, sin, pad], axis=1)
    return jnp.tile(cos_h, (1, N_HEADS)), jnp.tile(sin_h, (1, N_HEADS))


def kernel(x, norm1_g, w_in, b_forget, conv_w, conv_b, conv_ln_g, conv_ln_b, mla_gq, mla_wuq, mla_gkv,
           mla_wukv, w_gate, b_gate, w_branch, w_out, norm2_g, w_router_group, b_router_group,
           w_router_expert, b_router_expert, w_exp_gate, w_exp_up, w_exp_down, final_g):
    batch, seq, d = x.shape
    t = batch * seq
    depth = w_in.shape[0]
    assert seq % MOBA_BLOCK == 0 and t % TOKEN_TILE == 0
    cos_t, sin_t = _rope_tables(seq)
    pk, pkr = _rope_placement()
    n_slots = 2 * t
    p_rows = n_slots + N_EXPERTS * FFN_ROWS
    n_blk = p_rows // FFN_ROWS

    h = x.reshape(t, d)
    moe = None
    for l in range(depth):
        w_l, bf_l = _prep_inproj(w_in[l], b_forget[l])
        outs = _inproj(h, norm1_g[l][None], w_l, bf_l, seq=seq, moe=moe)
        if moe is not None:
            h, *outs = outs
        moba, moba_vt, fox, fox_vt, conv_in, cq, ckv, small, cumt = outs
        y_a = _attention("moba", moba, 0, moba, 1, moba_vt, batch=batch, seq=seq)
        y_b = _attention("fox", fox, 0, fox, 1, fox_vt, batch=batch, seq=seq, extra=(small, cumt))
        conv_wp = jnp.zeros((32, CONV_CH), F32).at[:CONV_K].set(conv_w[l])
        conv_p = jnp.zeros((SUBLANES, CONV_CH), F32).at[0].set(conv_b[l]).at[1].set(conv_ln_g[l]).at[2].set(
            conv_ln_b[l])
        y_c = _conv_module(conv_in, conv_wp, conv_p, batch=batch, seq=seq)
        wq, wqr, wk, wv = _prep_mla(mla_wuq[l], mla_wukv[l])
        qd, kd, vd = _mla_prep(cq, ckv, small, cos_t, sin_t, mla_gq[l][None], mla_gkv[l][None],
                               wq, wqr, wk, wv, pk, pkr, seq=seq)
        y_d = _attention("mla", qd, 0, kd, 0, vd, batch=batch, seq=seq)
        h = _merge(h, norm1_g[l][None], (y_a, y_b, y_c, y_d), w_gate[l].astype(BF16), b_gate[l][:, None, :],
                   w_branch[l].astype(BF16), w_out[l].astype(BF16))

        wr = jnp.zeros((d, LANES), F32).at[:, :N_EXPERTS].set(w_router_expert[l]).at[
            :, N_EXPERTS:N_EXPERTS + N_GROUPS].set(w_router_group[l])
        br = jnp.zeros((1, LANES), F32).at[0, :N_EXPERTS].set(b_router_expert[l]).at[
            0, N_EXPERTS:N_EXPERTS + N_GROUPS].set(b_router_group[l])
        hn, ri, rw, cnt = _route(h, norm2_g[l][None], wr, br)
        counts = cnt[0, :N_EXPERTS].astype(I32)
        starts = jnp.cumsum(counts) - counts
        pcounts = ((counts + FFN_ROWS - 1) // FFN_ROWS) * FFN_ROWS
        pends = jnp.cumsum(pcounts)
        pstarts = pends - pcounts
        order = jnp.argsort(ri[:, 0:2].T.reshape(-1)).astype(I32)
        blk_start = jnp.arange(n_blk, dtype=I32) * FFN_ROWS
        blk_e = jnp.minimum(jnp.sum(blk_start[:, None] >= pends[None, :], axis=1), N_EXPERTS - 1).astype(I32)
        pos = jnp.arange(p_rows, dtype=I32)
        e_p = jnp.repeat(blk_e, FFN_ROWS)
        off = pos - pstarts[e_p]
        real = off < counts[e_p]
        before = starts[e_p] + jnp.minimum(off, counts[e_p])
        slot = order[jnp.minimum(before, n_slots - 1)]
        src_tok = jnp.where(real, slot % t, 0)
        dst_row = jnp.where(real, slot, n_slots + pos - before)
        w13 = jnp.concatenate([w_exp_gate[l], w_exp_up[l]], axis=-1).astype(BF16)
        y = _expert_ffn(hn, src_tok.reshape(n_blk, 1, FFN_ROWS), dst_row.reshape(n_blk, 1, FFN_ROWS), blk_e,
                        w13, w_exp_down[l].astype(BF16), y_rows=p_rows)
        moe = (y, rw)
    return _final_norm(h, *moe, final_g[None]).reshape(batch, seq, d)
```
